```python
import math
import jax
import jax.numpy as jnp
from jax import lax
import numpy as np

D_MODEL = 1024
BATCH = 8
SEQ = 2048
DEPTH = 2
DEC_BATCH = 32
DEC_SEQ = 1
PAST_LEN = 8192
PAGE_SIZE = 128

HEAD_DIM = 64
MOBA_HEADS = D_MODEL // 128
MOBA_BLOCK = 256
MOBA_TOPK = 3
SB_HEADS = D_MODEL // 128
LRU_WIDTH = D_MODEL // 2
LRU_BLOCKS = 8
LRU_BW = LRU_WIDTH // LRU_BLOCKS
CONV_W = 4
RG_C = 8.0
N_BUCKETS = 32
MAX_DIST = 128
D_FF = ((8 * D_MODEL // 3 + 127) // 128) * 128
N_BRANCH = 3
QBLK = 128
MOBA_W = MOBA_HEADS * HEAD_DIM
SB_W = SB_HEADS * HEAD_DIM
IN_COLS = 3 * MOBA_W + 2 * LRU_WIDTH + 3 * SB_W + N_BRANCH * D_MODEL
RMS_EPS = 1e-6
NEG = -1e30

kernel_name = 'hybrid_moba_rglru_stickbreak_decode_step'


def rmsnorm(x, g):
    xf = x.astype(jnp.float32)
    y = xf * lax.rsqrt(jnp.mean(xf * xf, axis=-1, keepdims=True) + RMS_EPS) * g.astype(jnp.float32)
    return y.astype(x.dtype)


def swiglu(x, w1, w3, w2):
    return (jax.nn.silu(x @ w1) * (x @ w3)) @ w2


def rel_bucket(dist):
    max_exact = N_BUCKETS // 2
    d = jnp.maximum(dist, 0)
    df = jnp.maximum(d, 1).astype(jnp.float32)
    large = max_exact + (jnp.log(df / max_exact) / math.log(MAX_DIST / max_exact)
                         * (N_BUCKETS - max_exact)).astype(jnp.int32)
    large = jnp.minimum(large, N_BUCKETS - 1)
    return jnp.where(d < max_exact, d, large)


def _lin_combine(left, right):
    a1, b1 = left
    a2, b2 = right
    return a1 * a2, a2 * b1 + b2


def moba_attention(q, q_start, k_all, v_all, rel_bias):
    B, Tq, H, D = q.shape
    Tk = k_all.shape[1]
    NB = -(-Tk // MOBA_BLOCK)
    pad = NB * MOBA_BLOCK - Tk
    kb = jnp.pad(k_all, ((0, 0), (0, pad), (0, 0), (0, 0))).reshape(B, NB, MOBA_BLOCK, H, D)
    vb = jnp.pad(v_all, ((0, 0), (0, pad), (0, 0), (0, 0))).reshape(B, NB, MOBA_BLOCK, H, D)
    kmean = jnp.mean(kb, axis=2, dtype=jnp.float32)
    n_sel = min(MOBA_TOPK, NB)
    scale = D ** -0.5
    hidx = jnp.arange(H)[None, :, None]
    offs = jnp.arange(MOBA_BLOCK, dtype=jnp.int32)

    def item(args):
        qc, b, qp = args
        kbb, vbb, kmb = kb[b], vb[b], kmean[b]
        C = qc.shape[0]
        blk = qp // MOBA_BLOCK
        gate = jnp.einsum('chd,nhd->chn', qc, kmb, preferred_element_type=jnp.float32)
        past_blk = jnp.arange(NB)[None, None, :] < blk[:, None, None]
        gate = jnp.where(past_blk, gate, NEG)
        _, sel = lax.top_k(gate, n_sel)
        sel_ok = sel < blk[:, None, None]
        k_sel = kbb[sel, :, hidx, :]
        v_sel = vbb[sel, :, hidx, :]
        s_sel = jnp.einsum('chd,chjpd->chjp', qc, k_sel, preferred_element_type=jnp.float32) * scale
        pos_sel = sel[..., None] * MOBA_BLOCK + offs
        b_sel = rel_bias[rel_bucket(qp[:, None, None, None] - pos_sel), hidx[..., None]]
        s_sel = jnp.where(sel_ok[..., None], s_sel + b_sel, NEG)
        k_own = kbb[blk]
        v_own = vbb[blk]
        s_own = jnp.einsum('chd,cphd->chp', qc, k_own, preferred_element_type=jnp.float32) * scale
        d_own = qp[:, None] - (blk[:, None] * MOBA_BLOCK + offs)
        b_own = jnp.transpose(rel_bias[rel_bucket(d_own)], (0, 2, 1))
        s_own = jnp.where((d_own >= 0)[:, None, :], s_own + b_own, NEG)
        n_k = n_sel * MOBA_BLOCK
        p = jax.nn.softmax(jnp.concatenate([s_sel.reshape(C, H, n_k), s_own], axis=-1), axis=-1)
        p_sel = p[..., :n_k].reshape(C, H, n_sel, MOBA_BLOCK).astype(v_sel.dtype)
        p_own = p[..., n_k:].astype(v_own.dtype)
        o = (jnp.einsum('chjp,chjpd->chd', p_sel, v_sel, preferred_element_type=jnp.float32)
             + jnp.einsum('chp,cphd->chd', p_own, v_own, preferred_element_type=jnp.float32))
        return o.astype(qc.dtype)

    C = QBLK if Tq % QBLK == 0 else Tq
    NC = Tq // C
    q_items = q.reshape(B * NC, C, H, D)
    b_items = jnp.repeat(jnp.arange(B, dtype=jnp.int32), NC)
    p_items = jnp.tile(q_start + jnp.arange(Tq, dtype=jnp.int32).reshape(NC, C), (B, 1))
    o = lax.map(item, (q_items, b_items, p_items))
    return o.reshape(B, Tq, H * D)


def stick_breaking(q, q_start, k_segs, v_segs, kpos_segs):
    B, Tq, H, D = q.shape
    kpos = jnp.concatenate(kpos_segs)
    split_at = [int(s) for s in np.cumsum([k.shape[1] for k in k_segs])[:-1]]
    scale = D ** -0.5

    def block(qc, qp):
        z = jnp.concatenate([jnp.einsum('bchd,bkhd->bhck', qc, k, preferred_element_type=jnp.float32)
                             for k in k_segs], axis=-1) * scale
        valid = (kpos[None, :] < qp[:, None])[None, None]
        log_keep = jnp.where(valid, jax.nn.log_sigmoid(-z), 0.0)
        log_between = lax.cumsum(log_keep, axis=3, reverse=True) - log_keep
        w = jnp.where(valid, jnp.exp(jax.nn.log_sigmoid(z) + log_between), 0.0)
        ws = jnp.split(w, split_at, axis=-1)
        o = sum(jnp.einsum('bhck,bkhd->bchd', wi.astype(v.dtype), v, preferred_element_type=jnp.float32)
                for wi, v in zip(ws, v_segs))
        return o.astype(qc.dtype)

    qpos = q_start + jnp.arange(Tq, dtype=jnp.int32)
    if Tq > QBLK and Tq % QBLK == 0:
        NC = Tq // QBLK
        qs = q.reshape(B, NC, QBLK, H, D).transpose(1, 0, 2, 3, 4)
        o = lax.map(lambda a: block(a[0], a[1]), (qs, qpos.reshape(NC, QBLK)))
        o = o.transpose(1, 0, 2, 3, 4)
    else:
        o = block(q, qpos)
    return o.reshape(B, Tq, H * D)


def rglru_branch(xb, gb, conv_buf, h0, conv_w, conv_b, wa, ba, wx, bx, lam):
    B, T, W = xb.shape
    xp = jnp.concatenate([conv_buf.astype(xb.dtype), xb], axis=1)
    xc = conv_b + sum(xp[:, i:i + T] * conv_w[i] for i in range(CONV_W))
    xh = xc.reshape(B, T, LRU_BLOCKS, LRU_BW)
    r = jax.nn.sigmoid(jnp.einsum('btnk,nkj->btnj', xh, wa, preferred_element_type=jnp.float32).reshape(B, T, W)
                       + ba.astype(jnp.float32))
    i = jax.nn.sigmoid(jnp.einsum('btnk,nkj->btnj', xh, wx, preferred_element_type=jnp.float32).reshape(B, T, W)
                       + bx.astype(jnp.float32))
    log_a = -RG_C * r * jax.nn.softplus(-lam.astype(jnp.float32))
    a = jnp.exp(log_a)
    b = jnp.sqrt(-jnp.expm1(2.0 * log_a)) * (i * xc.astype(jnp.float32))
    b = b.at[:, 0].add(a[:, 0] * h0.astype(jnp.float32))
    _, hs = lax.associative_scan(_lin_combine, (a, b), axis=1)
    y = hs * jax.nn.gelu(gb.astype(jnp.float32))
    return y.astype(xb.dtype), xp[:, T:], hs[:, -1].astype(h0.dtype)


def decoder_layer(x, start, past, conv_buf, h0, rel_bias, lw):
    B, T, _ = x.shape
    x = x + 0.5 * swiglu(rmsnorm(x, lw['ln1']), lw['f1_w1'], lw['f1_w3'], lw['f1_w2'])
    h = rmsnorm(x, lw['ln_mix'])
    sizes = [MOBA_W, MOBA_W, MOBA_W, LRU_WIDTH, LRU_WIDTH, SB_W, SB_W, SB_W]
    qa, ka, va, xb, gb, qc, kc, vc, gl = jnp.split(h @ lw['w_in'], [int(s) for s in np.cumsum(sizes)], axis=-1)
    qa = qa.reshape(B, T, MOBA_HEADS, HEAD_DIM)
    ka = ka.reshape(B, T, MOBA_HEADS, HEAD_DIM)
    va = va.reshape(B, T, MOBA_HEADS, HEAD_DIM)
    qc = qc.reshape(B, T, SB_HEADS, HEAD_DIM)
    kc = kc.reshape(B, T, SB_HEADS, HEAD_DIM)
    vc = vc.reshape(B, T, SB_HEADS, HEAD_DIM)
    new_pos = start + jnp.arange(T, dtype=jnp.int32)
    if past is None:
        ka_all, va_all = ka, va
        k_segs, v_segs, p_segs = [kc], [vc], [new_pos]
    else:
        pk_a, pv_a, pk_c, pv_c = past
        ka_all = jnp.concatenate([pk_a, ka], axis=1)
        va_all = jnp.concatenate([pv_a, va], axis=1)
        k_segs, v_segs = [pk_c, kc], [pv_c, vc]
        p_segs = [jnp.arange(pk_c.shape[1], dtype=jnp.int32), new_pos]
    o_a = moba_attention(qa, start, ka_all, va_all, rel_bias)
    o_b, conv_new, h_new = rglru_branch(xb, gb, conv_buf, h0, lw['conv_w'], lw['conv_b'],
                                        lw['wa'], lw['ba'], lw['wx'], lw['bx'], lw['lam'])
    o_c = stick_breaking(qc, start, k_segs, v_segs, p_segs)
    branches = jnp.stack([o_a, o_b, o_c], axis=2)
    proj = jnp.einsum('btnw,nwd->btnd', branches, lw['w_branch'])
    gates = jax.nn.sigmoid((gl.reshape(B, T, N_BRANCH, D_MODEL) + lw['b_gate']).astype(jnp.float32))
    merged = jnp.sum(gates * proj, axis=2).astype(x.dtype)
    x = x + merged @ lw['w_out']
    x = x + 0.5 * swiglu(rmsnorm(x, lw['ln2']), lw['f2_w1'], lw['f2_w3'], lw['f2_w2'])
    return x, ka, va, kc, vc, conv_new, h_new


def setup_inputs(seed: int = 0) -> dict:
    key = jax.random.key(seed)
    ks = jax.random.split(key, 32)
    f32 = jnp.float32
    n_pages = PAST_LEN // PAGE_SIZE
    n_phys = (DEC_BATCH * n_pages * 5 + 3) // 4

    def nrm(k, shape, s):
        return s * jax.random.normal(k, shape, f32)

    page_table = jax.random.permutation(ks[6], n_phys)[:DEC_BATCH * n_pages].reshape(DEC_BATCH, n_pages).astype(jnp.int32)
    a0 = jax.random.uniform(ks[19], (DEPTH, LRU_WIDTH), f32, 0.9, 0.999) ** (1.0 / RG_C)
    lru_lambda = jnp.log(a0) - jnp.log1p(-a0)
    return {
        'x_prompt': nrm(ks[0], (BATCH, SEQ, D_MODEL), 1.0),
        'x_sample': nrm(ks[1], (DEC_BATCH, DEC_SEQ, D_MODEL), 1.0),
        'cache_moba_k': nrm(ks[2], (DEPTH, n_phys, PAGE_SIZE, MOBA_HEADS, HEAD_DIM), 1.0),
        'cache_moba_v': nrm(ks[3], (DEPTH, n_phys, PAGE_SIZE, MOBA_HEADS, HEAD_DIM), 1.0),
        'cache_sb_k': nrm(ks[4], (DEPTH, n_phys, PAGE_SIZE, SB_HEADS, HEAD_DIM), 1.0),
        'cache_sb_v': nrm(ks[5], (DEPTH, n_phys, PAGE_SIZE, SB_HEADS, HEAD_DIM), 1.0),
        'state_lru_h': nrm(ks[7], (DEPTH, DEC_BATCH, LRU_WIDTH), 0.5),
        'state_lru_conv': nrm(ks[8], (DEPTH, DEC_BATCH, CONV_W - 1, LRU_WIDTH), 1.0),
        'page_table': page_table,
        'rel_bias': nrm(ks[9], (N_BUCKETS, MOBA_HEADS), 0.5),
        'norm_ffn1': 1.0 + nrm(ks[10], (DEPTH, D_MODEL), 0.05),
        'ffn1_w1': nrm(ks[11], (DEPTH, D_MODEL, D_FF), D_MODEL ** -0.5),
        'ffn1_w3': nrm(ks[12], (DEPTH, D_MODEL, D_FF), D_MODEL ** -0.5),
        'ffn1_w2': nrm(ks[13], (DEPTH, D_FF, D_MODEL), D_FF ** -0.5),
        'norm_mix': 1.0 + nrm(ks[14], (DEPTH, D_MODEL), 0.05),
        'w_in': nrm(ks[15], (DEPTH, D_MODEL, IN_COLS), D_MODEL ** -0.5),
        'b_gate': nrm(ks[16], (DEPTH, N_BRANCH, D_MODEL), 0.1),
        'conv_w': nrm(ks[17], (DEPTH, CONV_W, LRU_WIDTH), CONV_W ** -0.5),
        'conv_b': nrm(ks[18], (DEPTH, LRU_WIDTH), 0.01),
        'lru_wa': nrm(ks[20], (DEPTH, LRU_BLOCKS, LRU_BW, LRU_BW), LRU_BW ** -0.5),
        'lru_ba': nrm(ks[21], (DEPTH, LRU_WIDTH), 0.1),
        'lru_wx': nrm(ks[22], (DEPTH, LRU_BLOCKS, LRU_BW, LRU_BW), LRU_BW ** -0.5),
        'lru_bx': nrm(ks[23], (DEPTH, LRU_WIDTH), 0.1),
        'lru_lambda': lru_lambda,
        'w_branch': nrm(ks[24], (DEPTH, N_BRANCH, MOBA_W, D_MODEL), MOBA_W ** -0.5),
        'w_out': nrm(ks[25], (DEPTH, D_MODEL, D_MODEL), D_MODEL ** -0.5),
        'norm_ffn2': 1.0 + nrm(ks[26], (DEPTH, D_MODEL), 0.05),
        'ffn2_w1': nrm(ks[27], (DEPTH, D_MODEL, D_FF), D_MODEL ** -0.5),
        'ffn2_w3': nrm(ks[28], (DEPTH, D_MODEL, D_FF), D_MODEL ** -0.5),
        'ffn2_w2': nrm(ks[29], (DEPTH, D_FF, D_MODEL), D_FF ** -0.5),
        'norm_final': 1.0 + nrm(ks[30], (D_MODEL,), 0.05),
    }


def reference(x_prompt, x_sample, cache_moba_k, cache_moba_v, cache_sb_k, cache_sb_v,
              state_lru_h, state_lru_conv, page_table, rel_bias,
              norm_ffn1, ffn1_w1, ffn1_w3, ffn1_w2, norm_mix, w_in, b_gate,
              conv_w, conv_b, lru_wa, lru_ba, lru_wx, lru_bx, lru_lambda,
              w_branch, w_out, norm_ffn2, ffn2_w1, ffn2_w3, ffn2_w2, norm_final):
    bp = x_prompt.shape[0]
    db = x_sample.shape[0]
    past_len = page_table.shape[1] * PAGE_SIZE
    yp, ys = x_prompt, x_sample
    p_mk, p_mv, p_sk, p_sv, p_h, p_cv = [], [], [], [], [], []
    s_mk, s_mv, s_sk, s_sv, s_h, s_cv = [], [], [], [], [], []
    for l in range(DEPTH):
        lw = {'ln1': norm_ffn1[l], 'f1_w1': ffn1_w1[l], 'f1_w3': ffn1_w3[l], 'f1_w2': ffn1_w2[l],
              'ln_mix': norm_mix[l], 'w_in': w_in[l], 'b_gate': b_gate[l],
              'conv_w': conv_w[l], 'conv_b': conv_b[l], 'wa': lru_wa[l], 'ba': lru_ba[l],
              'wx': lru_wx[l], 'bx': lru_bx[l], 'lam': lru_lambda[l],
              'w_branch': w_branch[l], 'w_out': w_out[l],
              'ln2': norm_ffn2[l], 'f2_w1': ffn2_w1[l], 'f2_w3': ffn2_w3[l], 'f2_w2': ffn2_w2[l]}
        yp, ka, va, kc, vc, cv, hl = decoder_layer(
            yp, 0, None,
            jnp.zeros((bp, CONV_W - 1, LRU_WIDTH), x_prompt.dtype),
            jnp.zeros((bp, LRU_WIDTH), x_prompt.dtype), rel_bias, lw)
        p_mk.append(ka); p_mv.append(va); p_sk.append(kc); p_sv.append(vc); p_cv.append(cv); p_h.append(hl)
        past = (cache_moba_k[l, page_table].reshape(db, past_len, MOBA_HEADS, HEAD_DIM),
                cache_moba_v[l, page_table].reshape(db, past_len, MOBA_HEADS, HEAD_DIM),
                cache_sb_k[l, page_table].reshape(db, past_len, SB_HEADS, HEAD_DIM),
                cache_sb_v[l, page_table].reshape(db, past_len, SB_HEADS, HEAD_DIM))
        ys, ka, va, kc, vc, cv, hl = decoder_layer(
            ys, past_len, past, state_lru_conv[l], state_lru_h[l], rel_bias, lw)
        s_mk.append(ka); s_mv.append(va); s_sk.append(kc); s_sv.append(vc); s_cv.append(cv); s_h.append(hl)
    yp = rmsnorm(yp, norm_final)
    ys = rmsnorm(ys, norm_final)
    return (yp, ys,
            jnp.stack(p_mk), jnp.stack(p_mv), jnp.stack(p_sk), jnp.stack(p_sv), jnp.stack(p_h), jnp.stack(p_cv),
            jnp.stack(s_mk), jnp.stack(s_mv), jnp.stack(s_sk), jnp.stack(s_sv), jnp.stack(s_h), jnp.stack(s_cv))
```

```python
import functools
import math

import jax
import jax.numpy as jnp
from jax import lax
from jax.experimental import pallas as pl
from jax.experimental.pallas import tpu as pltpu

F32 = jnp.float32
BF16 = jnp.bfloat16

HEAD_DIM = 64
N_HEADS = 8
ATT_W = N_HEADS * HEAD_DIM
LANES = 128
HEADS_PER_TILE = LANES // HEAD_DIM
N_PAIRS = ATT_W // LANES
MOBA_BLOCK = 256
MOBA_TOPK = 3
PAGE_SIZE = 128
PAGES_PER_MOBA_BLOCK = MOBA_BLOCK // PAGE_SIZE
PAGES_PER_STEP = 8
LRU_W = 512
LRU_BLOCKS = 8
CONV_W = 4
RG_C = 8.0
N_BUCKETS = 32
MAX_DIST = 128
RMS_EPS = 1e-6
NEG = -1e30
SCALE = HEAD_DIM ** -0.5
VMEM_LIMIT = 56 * 1024 * 1024

_NT = (((1,), (1,)), ((), ()))


def _params(*sem):
    return pltpu.CompilerParams(dimension_semantics=sem, vmem_limit_bytes=VMEM_LIMIT)


def _rms(x, g):
    return x * lax.rsqrt(jnp.mean(x * x, axis=-1, keepdims=True) + RMS_EPS) * g


def _softplus(z):
    return jnp.maximum(z, 0.0) + jnp.log1p(jnp.exp(-jnp.abs(z)))


def _expm1(x):
    u = jnp.exp(x)
    um1 = u - 1.0
    return jnp.where(um1 == 0.0, x, jnp.where(um1 == -1.0, -1.0, um1 * x / jnp.log(u)))


def _dot_nt(a, b):
    return lax.dot_general(a.astype(BF16), b.astype(BF16), _NT, preferred_element_type=F32)


def _split_dot(x, u):
    hi = x.astype(BF16)
    lo = (x - hi.astype(F32)).astype(BF16)
    return (jnp.dot(hi, u, preferred_element_type=F32)
            + jnp.dot(lo, u, preferred_element_type=F32))


def _ffn_kernel(x_ref, g_ref, w1_ref, w3_ref, w2_ref, gf_ref, o_ref, h_ref, acc_ref, *, final_norm):
    j = pl.program_id(1)

    @pl.when(j == 0)
    def _():
        h_ref[...] = _rms(x_ref[...], g_ref[...]).astype(BF16)
        acc_ref[...] = jnp.zeros_like(acc_ref)

    h = h_ref[...]
    a = jnp.dot(h, w1_ref[...], preferred_element_type=F32)
    b = jnp.dot(h, w3_ref[...], preferred_element_type=F32)
    u = a * jax.nn.sigmoid(a) * b
    acc_ref[...] += jnp.dot(u.astype(BF16), w2_ref[...], preferred_element_type=F32)

    @pl.when(j == pl.num_programs(1) - 1)
    def _():
        y = x_ref[...] + 0.5 * acc_ref[...]
        if final_norm:
            y = _rms(y, gf_ref[...])
        o_ref[...] = y


def _ffn(x, g, w1, w3, w2, gf, *, final_norm, tm, tf):
    n, d = x.shape
    dff = w1.shape[1]
    return pl.pallas_call(
        functools.partial(_ffn_kernel, final_norm=final_norm),
        grid=(n // tm, dff // tf),
        in_specs=[
            pl.BlockSpec((tm, d), lambda i, j: (i, 0)),
            pl.BlockSpec((1, d), lambda i, j: (0, 0)),
            pl.BlockSpec((d, tf), lambda i, j: (0, j)),
            pl.BlockSpec((d, tf), lambda i, j: (0, j)),
            pl.BlockSpec((tf, d), lambda i, j: (j, 0)),
            pl.BlockSpec((1, d), lambda i, j: (0, 0)),
        ],
        out_specs=pl.BlockSpec((tm, d), lambda i, j: (i, 0)),
        out_shape=jax.ShapeDtypeStruct((n, d), F32),
        scratch_shapes=[pltpu.VMEM((tm, d), BF16), pltpu.VMEM((tm, d), F32)],
        compiler_params=_params("parallel", "arbitrary"),
        name="ffn",
    )(x, g, w1, w3, w2, gf)


N_IN_SPLIT = 8


def _inproj_kernel(x_ref, g_ref, w_ref, *refs):
    outs, h_ref = refs[:-1], refs[-1]
    j = pl.program_id(1)

    @pl.when(j == 0)
    def _():
        h_ref[...] = _rms(x_ref[...], g_ref[...]).astype(BF16)

    p = jnp.dot(h_ref[...], w_ref[...], preferred_element_type=F32)
    for k in range(N_IN_SPLIT):
        @pl.when(j == k)
        def _(k=k):
            outs[k][...] = p

    @pl.when(j >= N_IN_SPLIT)
    def _():
        outs[N_IN_SPLIT][...] = p


def _inproj(x, g, w_in, *, tm):
    n, d = x.shape
    cols = w_in.shape[1]
    n_gate_tiles = cols // ATT_W - N_IN_SPLIT
    out_specs = [pl.BlockSpec((tm, ATT_W), lambda i, j: (i, 0)) for _ in range(N_IN_SPLIT)]
    out_specs.append(pl.BlockSpec((tm, ATT_W), lambda i, j: (i, jnp.maximum(j - N_IN_SPLIT, 0))))
    out_shape = [jax.ShapeDtypeStruct((n, ATT_W), F32) for _ in range(N_IN_SPLIT)]
    out_shape.append(jax.ShapeDtypeStruct((n, n_gate_tiles * ATT_W), F32))
    return pl.pallas_call(
        _inproj_kernel,
        grid=(n // tm, cols // ATT_W),
        in_specs=[
            pl.BlockSpec((tm, d), lambda i, j: (i, 0)),
            pl.BlockSpec((1, d), lambda i, j: (0, 0)),
            pl.BlockSpec((d, ATT_W), lambda i, j: (0, j)),
        ],
        out_specs=out_specs,
        out_shape=out_shape,
        scratch_shapes=[pltpu.VMEM((tm, d), BF16)],
        compiler_params=_params("parallel", "arbitrary"),
        name="inproj",
    )(x, g, w_in)


def _merge_kernel(oa_ref, ob_ref, oc_ref, gl_ref, x_ref, bg_ref, wb_ref, wo_ref, o_ref):
    d = x_ref.shape[1]
    merged = None
    for n, o_n in enumerate((oa_ref, ob_ref, oc_ref)):
        proj = jnp.dot(o_n[...].astype(BF16), wb_ref[n], preferred_element_type=F32)
        gate = jax.nn.sigmoid(gl_ref[:, n * d:(n + 1) * d] + bg_ref[n:n + 1, :])
        merged = gate * proj if merged is None else merged + gate * proj
    o_ref[...] = x_ref[...] + jnp.dot(merged.astype(BF16), wo_ref[...], preferred_element_type=F32)


def _merge(oa, ob, oc, gl, x, bg, wb, wo, *, tm):
    n, d = x.shape
    row = lambda w: pl.BlockSpec((tm, w), lambda i: (i, 0))
    return pl.pallas_call(
        _merge_kernel,
        grid=(n // tm,),
        in_specs=[row(ATT_W), row(ATT_W), row(ATT_W), row(gl.shape[1]), row(d),
                  pl.BlockSpec(bg.shape, lambda i: (0, 0)),
                  pl.BlockSpec(wb.shape, lambda i: (0, 0, 0)),
                  pl.BlockSpec(wo.shape, lambda i: (0, 0))],
        out_specs=row(d),
        out_shape=jax.ShapeDtypeStruct((n, d), F32),
        compiler_params=_params("parallel"),
        name="merge",
    )(oa, ob, oc, gl, x, bg, wb, wo)


def _rel_bucket(dist):
    max_exact = N_BUCKETS // 2
    d = jnp.maximum(dist, 0)
    df = jnp.maximum(d, 1).astype(F32)
    large = max_exact + (jnp.log(df / max_exact) / math.log(MAX_DIST / max_exact)
                         * (N_BUCKETS - max_exact)).astype(jnp.int32)
    large = jnp.minimum(large, N_BUCKETS - 1)
    return jnp.where(d < max_exact, d, large)


def _bias_tables(rel_bias):
    r = jnp.arange(MOBA_BLOCK, dtype=jnp.int32)[:, None]
    c = jnp.arange(2 * MOBA_BLOCK, dtype=jnp.int32)[None, :]
    toeplitz = rel_bias[_rel_bucket(MOBA_BLOCK + r - c)]
    toeplitz = jnp.transpose(toeplitz, (2, 0, 1))
    far = rel_bias[_rel_bucket(jnp.int32(2 * MOBA_BLOCK))]
    far = jnp.broadcast_to(far[:, None, None], (N_HEADS, 1, LANES))
    dec = rel_bias[_rel_bucket(MOBA_BLOCK - jnp.arange(MOBA_BLOCK, dtype=jnp.int32))].T
    own0 = rel_bias[_rel_bucket(jnp.int32(0))][:, None]
    return toeplitz, far, dec, own0


def _moba_prompt_kernel(q_ref, k_ref, v_ref, tb_ref, far_ref, o_ref, kmean_ref):
    i = pl.program_id(2)
    nb = k_ref.shape[0] // MOBA_BLOCK

    @pl.when(i == 0)
    def _():
        for n in range(nb):
            kmean_ref[n:n + 1, :] = jnp.mean(k_ref[n * MOBA_BLOCK:(n + 1) * MOBA_BLOCK, :],
                                             axis=0, keepdims=True)

    q = q_ref[...]
    lane = lax.broadcasted_iota(jnp.int32, (1, LANES), 1)
    blk_lane = lax.broadcasted_iota(jnp.int32, (1, nb), 1)
    row_i = lax.broadcasted_iota(jnp.int32, (MOBA_BLOCK, MOBA_BLOCK), 0)
    col_i = lax.broadcasted_iota(jnp.int32, (MOBA_BLOCK, MOBA_BLOCK), 1)
    own_start = pl.multiple_of(i * MOBA_BLOCK, MOBA_BLOCK)
    k_own = k_ref[pl.ds(own_start, MOBA_BLOCK), :].astype(BF16)
    v_own = v_ref[pl.ds(own_start, MOBA_BLOCK), :].astype(BF16)
    kmean = kmean_ref[...]

    outs = []
    for hh in range(HEADS_PER_TILE):
        in_head = (lane // HEAD_DIM) == hh
        qh = jnp.where(in_head, q, 0.0)
        qh_bf = qh.astype(BF16)

        gate = lax.dot_general(qh, kmean, _NT, precision=lax.Precision.HIGHEST,
                               preferred_element_type=F32)
        gate = jnp.where(blk_lane < i, gate, NEG)
        rank = jnp.zeros_like(gate)
        for m in range(nb):
            gm = gate[:, m:m + 1]
            ahead = (gm > gate) | ((gm == gate) & (blk_lane > m))
            rank = rank + ahead.astype(F32)
        sel = ((rank < MOBA_TOPK) & (blk_lane < i)).astype(F32)

        s = _dot_nt(qh_bf, k_own) * SCALE + tb_ref[hh, :, MOBA_BLOCK:]
        s = jnp.where(col_i <= row_i, s, NEG)
        m0 = jnp.max(s, axis=-1, keepdims=True)
        p = jnp.exp(s - m0)
        l0 = jnp.sum(p, axis=-1, keepdims=True)
        acc0 = jnp.dot(p.astype(BF16), v_own, preferred_element_type=F32)
        far = far_ref[hh]
        bias_adj = tb_ref[hh, :, :MOBA_BLOCK]
        bias_far = far[:, :1]

        def body(n, carry):
            m_run, l_run, acc = carry
            start = pl.multiple_of(n * MOBA_BLOCK, MOBA_BLOCK)
            kn = k_ref[pl.ds(start, MOBA_BLOCK), :].astype(BF16)
            vn = v_ref[pl.ds(start, MOBA_BLOCK), :].astype(BF16)
            picked = jnp.sum(jnp.where(blk_lane == n, sel, 0.0), axis=-1, keepdims=True)
            bias = jnp.where(n == i - 1, bias_adj, bias_far)
            sn = _dot_nt(qh_bf, kn) * SCALE + bias
            sn = jnp.where(picked > 0.5, sn, NEG)
            m_new = jnp.maximum(m_run, jnp.max(sn, axis=-1, keepdims=True))
            alpha = jnp.exp(m_run - m_new)
            pn = jnp.exp(sn - m_new)
            l_new = alpha * l_run + jnp.sum(pn, axis=-1, keepdims=True)
            acc_new = alpha * acc + jnp.dot(pn.astype(BF16), vn, preferred_element_type=F32)
            return m_new, l_new, acc_new

        _, l_fin, acc_fin = lax.fori_loop(0, i, body, (m0, l0, acc0))
        outs.append(acc_fin / l_fin)

    o_ref[...] = jnp.where((lane // HEAD_DIM) == 0, outs[0], outs[1])


def _moba_prompt(q, k, v, toeplitz, far, *, batch, seq):
    nb = seq // MOBA_BLOCK
    return pl.pallas_call(
        _moba_prompt_kernel,
        grid=(batch, N_PAIRS, nb),
        in_specs=[
            pl.BlockSpec((MOBA_BLOCK, LANES), lambda b, p, i: (b * nb + i, p)),
            pl.BlockSpec((seq, LANES), lambda b, p, i: (b, p)),
            pl.BlockSpec((seq, LANES), lambda b, p, i: (b, p)),
            pl.BlockSpec((HEADS_PER_TILE, MOBA_BLOCK, 2 * MOBA_BLOCK), lambda b, p, i: (p, 0, 0)),
            pl.BlockSpec((HEADS_PER_TILE, 1, LANES), lambda b, p, i: (p, 0, 0)),
        ],
        out_specs=pl.BlockSpec((MOBA_BLOCK, LANES), lambda b, p, i: (b * nb + i, p)),
        out_shape=jax.ShapeDtypeStruct(q.shape, F32),
        scratch_shapes=[pltpu.VMEM((nb, LANES), F32)],
        compiler_params=_params("parallel", "parallel", "arbitrary"),
        name="moba_prompt",
    )(q, k, v, toeplitz, far)


SB_BLOCK = 256


def _sb_prompt_kernel(q_ref, k_ref, v_ref, o_ref):
    i = pl.program_id(2)
    q = q_ref[...]
    lane = lax.broadcasted_iota(jnp.int32, (1, LANES), 1)
    row_i = lax.broadcasted_iota(jnp.int32, (SB_BLOCK, SB_BLOCK), 0)
    col_i = lax.broadcasted_iota(jnp.int32, (SB_BLOCK, SB_BLOCK), 1)
    later = (row_i > col_i).astype(BF16)
    valid = col_i < row_i
    own_start = pl.multiple_of(i * SB_BLOCK, SB_BLOCK)
    k_own = k_ref[pl.ds(own_start, SB_BLOCK), :].astype(BF16)
    v_own = v_ref[pl.ds(own_start, SB_BLOCK), :].astype(BF16)

    outs = []
    for hh in range(HEADS_PER_TILE):
        qh = jnp.where((lane // HEAD_DIM) == hh, q, 0.0).astype(BF16)

        z = _dot_nt(qh, k_own) * SCALE
        sp = _softplus(z)
        log_keep = jnp.where(valid, -sp, 0.0)
        between = _split_dot(log_keep, later)
        w = jnp.where(valid, jnp.exp(z - sp + between), 0.0)
        acc0 = jnp.dot(w.astype(BF16), v_own, preferred_element_type=F32)
        carry0 = jnp.sum(log_keep, axis=-1, keepdims=True)

        def body(t, state):
            carry, acc = state
            start = pl.multiple_of((i - 1 - t) * SB_BLOCK, SB_BLOCK)
            kn = k_ref[pl.ds(start, SB_BLOCK), :].astype(BF16)
            vn = v_ref[pl.ds(start, SB_BLOCK), :].astype(BF16)
            zn = _dot_nt(qh, kn) * SCALE
            spn = _softplus(zn)
            keep = -spn
            btw = _split_dot(keep, later) + carry
            wn = jnp.exp(zn - spn + btw)
            acc = acc + jnp.dot(wn.astype(BF16), vn, preferred_element_type=F32)
            return carry + jnp.sum(keep, axis=-1, keepdims=True), acc

        _, acc_fin = lax.fori_loop(0, i, body, (carry0, acc0))
        outs.append(acc_fin)

    o_ref[...] = jnp.where((lane // HEAD_DIM) == 0, outs[0], outs[1])


def _sb_prompt(q, k, v, *, batch, seq):
    nb = seq // SB_BLOCK
    return pl.pallas_call(
        _sb_prompt_kernel,
        grid=(batch, N_PAIRS, nb),
        in_specs=[
            pl.BlockSpec((SB_BLOCK, LANES), lambda b, p, i: (b * nb + i, p)),
            pl.BlockSpec((seq, LANES), lambda b, p, i: (b, p)),
            pl.BlockSpec((seq, LANES), lambda b, p, i: (b, p)),
        ],
        out_specs=pl.BlockSpec((SB_BLOCK, LANES), lambda b, p, i: (b * nb + i, p)),
        out_shape=jax.ShapeDtypeStruct(q.shape, F32),
        compiler_params=_params("parallel", "parallel", "arbitrary"),
        name="sb_prompt",
    )(q, k, v)


def _lru_gates(xc, wa_ref, ba_ref, wx_ref, bx_ref, lam_ref):
    xc_bf = xc.astype(BF16)
    r = jax.nn.sigmoid(jnp.dot(xc_bf, wa_ref[...], preferred_element_type=F32) + ba_ref[...])
    gi = jax.nn.sigmoid(jnp.dot(xc_bf, wx_ref[...], preferred_element_type=F32) + bx_ref[...])
    log_a = -RG_C * r * _softplus(-lam_ref[...])
    a = jnp.exp(log_a)
    b = jnp.sqrt(-_expm1(2.0 * log_a)) * (gi * xc)
    return a, b


def _lru_prompt_kernel(x_ref, g_ref, buf_ref, h0_ref, cw_ref, cb_ref, wa_ref, ba_ref, wx_ref, bx_ref,
                       lam_ref, y_ref, hlast_ref, conv_ref, tail_ref, hc_ref):
    t = pl.program_id(1)
    tc = x_ref.shape[0]
    tail_rows = tail_ref.shape[0]

    @pl.when(t == 0)
    def _():
        tail_ref[...] = jnp.zeros_like(tail_ref)
        tail_ref[tail_rows - (CONV_W - 1):, :] = buf_ref[...]
        hc_ref[...] = h0_ref[...]

    x = x_ref[...]
    xe = jnp.concatenate([tail_ref[...], x], axis=0)
    xc = cb_ref[...] + cw_ref[CONV_W - 1:CONV_W, :] * x
    for k in range(1, CONV_W):
        xc = xc + cw_ref[CONV_W - 1 - k:CONV_W - k, :] * pltpu.roll(xe, k, 0)[tail_rows:, :]
    a, b = _lru_gates(xc, wa_ref, ba_ref, wx_ref, bx_ref, lam_ref)

    row = lax.broadcasted_iota(jnp.int32, (tc, 1), 0)
    k = 1
    while k < tc:
        a_sh = jnp.where(row >= k, pltpu.roll(a, k, 0), 1.0)
        b_sh = jnp.where(row >= k, pltpu.roll(b, k, 0), 0.0)
        b = a * b_sh + b
        a = a * a_sh
        k *= 2
    h = a * hc_ref[...] + b
    y_ref[...] = h * jax.nn.gelu(g_ref[...], approximate=True)
    hc_ref[...] = h[tc - 1:tc, :]
    tail_ref[...] = x[tc - tail_rows:, :]

    @pl.when(t == pl.num_programs(1) - 1)
    def _():
        hlast_ref[...] = h[tc - 1:tc, :]
        conv_ref[...] = x[tc - (CONV_W - 1):, :]


def _lru_prompt(x, g, buf, h0, lw, *, batch, seq, tc):
    nt = seq // tc
    w = x.shape[1]
    vec = pl.BlockSpec((1, w), lambda b, t: (0, 0))
    mat = pl.BlockSpec((w, w), lambda b, t: (0, 0))
    return pl.pallas_call(
        _lru_prompt_kernel,
        grid=(batch, nt),
        in_specs=[
            pl.BlockSpec((tc, w), lambda b, t: (b * nt + t, 0)),
            pl.BlockSpec((tc, w), lambda b, t: (b * nt + t, 0)),
            pl.BlockSpec((None, CONV_W - 1, w), lambda b, t: (b, 0, 0)),
            pl.BlockSpec((None, 1, w), lambda b, t: (b, 0, 0)),
            pl.BlockSpec((CONV_W, w), lambda b, t: (0, 0)),
            vec, mat, vec, mat, vec, vec,
        ],
        out_specs=[
            pl.BlockSpec((tc, w), lambda b, t: (b * nt + t, 0)),
            pl.BlockSpec((None, 1, w), lambda b, t: (b, 0, 0)),
            pl.BlockSpec((None, CONV_W - 1, w), lambda b, t: (b, 0, 0)),
        ],
        out_shape=[jax.ShapeDtypeStruct(x.shape, F32),
                   jax.ShapeDtypeStruct((batch, 1, w), F32),
                   jax.ShapeDtypeStruct((batch, CONV_W - 1, w), F32)],
        scratch_shapes=[pltpu.VMEM((8, w), F32), pltpu.VMEM((1, w), F32)],
        compiler_params=_params("parallel", "arbitrary"),
        name="lru_prompt",
    )(x, g, buf, h0, lw["conv_w"], lw["conv_b"], lw["wa"], lw["ba"], lw["wx"], lw["bx"], lw["lam"])


def _lru_step_kernel(x_ref, g_ref, buf_ref, h0_ref, cw_ref, cb_ref, wa_ref, ba_ref, wx_ref, bx_ref,
                     lam_ref, y_ref, h_ref, conv_ref):
    x = x_ref[...]
    xc = cb_ref[...] + cw_ref[CONV_W - 1:CONV_W, :] * x
    for k in range(CONV_W - 1):
        xc = xc + cw_ref[k:k + 1, :] * buf_ref[k]
    a, b = _lru_gates(xc, wa_ref, ba_ref, wx_ref, bx_ref, lam_ref)
    h = a * h0_ref[...] + b
    y_ref[...] = h * jax.nn.gelu(g_ref[...], approximate=True)
    h_ref[...] = h
    for k in range(CONV_W - 2):
        conv_ref[k] = buf_ref[k + 1]
    conv_ref[CONV_W - 2] = x


def _lru_step(x, g, buf, h0, lw):
    n, w = x.shape
    return pl.pallas_call(
        _lru_step_kernel,
        out_shape=[jax.ShapeDtypeStruct((n, w), F32), jax.ShapeDtypeStruct((n, w), F32),
                   jax.ShapeDtypeStruct((CONV_W - 1, n, w), F32)],
        compiler_params=pltpu.CompilerParams(vmem_limit_bytes=VMEM_LIMIT),
        name="lru_step",
    )(x, g, buf, h0, lw["conv_w"], lw["conv_b"], lw["wa"], lw["ba"], lw["wx"], lw["bx"], lw["lam"])


def _head_rows(q_row):
    head_of_lane = lax.broadcasted_iota(jnp.int32, (N_HEADS, ATT_W), 1) // HEAD_DIM
    head_of_row = lax.broadcasted_iota(jnp.int32, (N_HEADS, ATT_W), 0)
    mask = head_of_lane == head_of_row
    return jnp.where(mask, jnp.broadcast_to(q_row, (N_HEADS, ATT_W)), 0.0), mask


def _moba_decode_kernel(pt_ref, q_ref, kn_ref, vn_ref, dec_ref, far_ref, own_ref, *refs):
    del pt_ref
    n_pg = PAGES_PER_STEP
    k_pages, v_pages = refs[:n_pg], refs[n_pg:2 * n_pg]
    o_ref, gate_ref, m_ref, l_ref, acc_ref = refs[2 * n_pg:]
    c = pl.program_id(1)
    n_steps = pl.num_programs(1)
    blocks_per_step = n_pg // PAGES_PER_MOBA_BLOCK
    qbd, head_mask = _head_rows(q_ref[...])
    qbd_bf = qbd.astype(BF16)
    lane = lax.broadcasted_iota(jnp.int32, (1, LANES), 1)

    @pl.when(c == 0)
    def _():
        gate_ref[...] = jnp.full_like(gate_ref, NEG)
        m_ref[...] = jnp.full_like(m_ref, NEG)
        l_ref[...] = jnp.zeros_like(l_ref)

    for j in range(blocks_per_step):
        n = c * blocks_per_step + j
        kb = jnp.concatenate([k_pages[PAGES_PER_MOBA_BLOCK * j + t][...]
                              for t in range(PAGES_PER_MOBA_BLOCK)], axis=0)
        vb = jnp.concatenate([v_pages[PAGES_PER_MOBA_BLOCK * j + t][...]
                              for t in range(PAGES_PER_MOBA_BLOCK)], axis=0)
        kmean = jnp.mean(kb, axis=0, keepdims=True)
        g = jnp.sum(qbd * kmean, axis=-1, keepdims=True)
        newest = n == n_steps * blocks_per_step - 1
        bias = jnp.where(newest, dec_ref[...], far_ref[...])
        s = _dot_nt(qbd_bf, kb) * SCALE + bias
        m = jnp.max(s, axis=-1, keepdims=True)
        p = jnp.exp(s - m)
        l = jnp.sum(p, axis=-1, keepdims=True)
        acc_ref[n] = jnp.dot(p.astype(BF16), vb.astype(BF16), preferred_element_type=F32)
        gate_ref[...] = jnp.where(lane == n, g, gate_ref[...])
        m_ref[...] = jnp.where(lane == n, m, m_ref[...])
        l_ref[...] = jnp.where(lane == n, l, l_ref[...])

    @pl.when(c == n_steps - 1)
    def _():
        nb = n_steps * blocks_per_step
        gate = gate_ref[...]
        rank = jnp.zeros_like(gate)
        for mm in range(nb):
            gm = gate[:, mm:mm + 1]
            ahead = (gm > gate) | ((gm == gate) & (lane > mm))
            rank = rank + ahead.astype(F32)
        sel = (rank < MOBA_TOPK) & (lane < nb)
        s_own = jnp.sum(qbd_bf.astype(F32) * kn_ref[...].astype(BF16).astype(F32), axis=-1,
                        keepdims=True) * SCALE + own_ref[...]
        m_sel = jnp.where(sel, m_ref[...], NEG)
        m_tot = jnp.maximum(jnp.max(m_sel, axis=-1, keepdims=True), s_own)
        wgt = jnp.where(sel, jnp.exp(m_sel - m_tot), 0.0)
        w_own = jnp.exp(s_own - m_tot)
        l_tot = jnp.sum(wgt * l_ref[...], axis=-1, keepdims=True) + w_own
        v_new = vn_ref[...].astype(BF16).astype(F32)
        o = w_own * jnp.broadcast_to(v_new, (N_HEADS, ATT_W))
        for mm in range(nb):
            o = o + wgt[:, mm:mm + 1] * acc_ref[mm]
        o = jnp.where(head_mask, o / l_tot, 0.0)
        o_ref[...] = jnp.sum(o, axis=0, keepdims=True)


def _page_spec(layer, slot, order):
    def index_map(b, c, pt):
        n_pages = pt.shape[1]
        pos = c * PAGES_PER_STEP + slot
        return (layer, pt[b, pos if order > 0 else n_pages - 1 - pos], 0, 0)
    return pl.BlockSpec((None, None, PAGE_SIZE, ATT_W), index_map)


def _moba_decode(page_table, q, k_new, v_new, cache_k, cache_v, dec, far, own0, *, layer):
    batch, n_pages = page_table.shape
    n_steps = n_pages // PAGES_PER_STEP
    nb = n_pages // PAGES_PER_MOBA_BLOCK
    row = pl.BlockSpec((None, 1, ATT_W), lambda b, c, pt: (b, 0, 0))
    const = lambda a: pl.BlockSpec(a.shape, lambda b, c, pt: (0,) * a.ndim)
    grid_spec = pltpu.PrefetchScalarGridSpec(
        num_scalar_prefetch=1,
        grid=(batch, n_steps),
        in_specs=[row, row, row, const(dec), const(far), const(own0)]
        + [_page_spec(layer, s, +1) for s in range(PAGES_PER_STEP)]
        + [_page_spec(layer, s, +1) for s in range(PAGES_PER_STEP)],
        out_specs=row,
        scratch_shapes=[pltpu.VMEM((N_HEADS, LANES), F32), pltpu.VMEM((N_HEADS, LANES), F32),
                        pltpu.VMEM((N_HEADS, LANES), F32), pltpu.VMEM((nb, N_HEADS, ATT_W), F32)],
    )
    r3 = lambda a: a.reshape(batch, 1, ATT_W)
    out = pl.pallas_call(
        _moba_decode_kernel,
        grid_spec=grid_spec,
        out_shape=jax.ShapeDtypeStruct((batch, 1, ATT_W), F32),
        compiler_params=_params("parallel", "arbitrary"),
        name="moba_decode",
    )(page_table, r3(q), r3(k_new), r3(v_new), dec, far, own0,
      *([cache_k] * PAGES_PER_STEP), *([cache_v] * PAGES_PER_STEP))
    return out.reshape(batch, ATT_W)


def _sb_decode_kernel(pt_ref, q_ref, *refs):
    del pt_ref
    n_pg = PAGES_PER_STEP
    k_pages, v_pages = refs[:n_pg], refs[n_pg:2 * n_pg]
    o_ref, carry_ref, acc_ref = refs[2 * n_pg:]
    c = pl.program_id(1)
    qbd, head_mask = _head_rows(q_ref[...])
    qbd_bf = qbd.astype(BF16)
    row_i = lax.broadcasted_iota(jnp.int32, (PAGE_SIZE, PAGE_SIZE), 0)
    col_i = lax.broadcasted_iota(jnp.int32, (PAGE_SIZE, PAGE_SIZE), 1)
    later = (row_i > col_i).astype(BF16)

    @pl.when(c == 0)
    def _():
        carry_ref[...] = jnp.zeros_like(carry_ref)
        acc_ref[...] = jnp.zeros_like(acc_ref)

    carry = carry_ref[...]
    acc = acc_ref[...]
    for j in range(n_pg):
        z = _dot_nt(qbd_bf, k_pages[j][...]) * SCALE
        sp = _softplus(z)
        keep = -sp
        between = _split_dot(keep, later) + carry
        w = jnp.exp(z - sp + between)
        acc = acc + jnp.dot(w.astype(BF16), v_pages[j][...].astype(BF16), preferred_element_type=F32)
        carry = carry + jnp.sum(keep, axis=-1, keepdims=True)
    carry_ref[...] = carry
    acc_ref[...] = acc

    @pl.when(c == pl.num_programs(1) - 1)
    def _():
        o_ref[...] = jnp.sum(jnp.where(head_mask, acc, 0.0), axis=0, keepdims=True)


def _sb_decode(page_table, q, cache_k, cache_v, *, layer):
    batch, n_pages = page_table.shape
    n_steps = n_pages // PAGES_PER_STEP
    row = pl.BlockSpec((None, 1, ATT_W), lambda b, c, pt: (b, 0, 0))
    grid_spec = pltpu.PrefetchScalarGridSpec(
        num_scalar_prefetch=1,
        grid=(batch, n_steps),
        in_specs=[row]
        + [_page_spec(layer, s, -1) for s in range(PAGES_PER_STEP)]
        + [_page_spec(layer, s, -1) for s in range(PAGES_PER_STEP)],
        out_specs=row,
        scratch_shapes=[pltpu.VMEM((N_HEADS, 1), F32), pltpu.VMEM((N_HEADS, ATT_W), F32)],
    )
    out = pl.pallas_call(
        _sb_decode_kernel,
        grid_spec=grid_spec,
        out_shape=jax.ShapeDtypeStruct((batch, 1, ATT_W), F32),
        compiler_params=_params("parallel", "arbitrary"),
        name="sb_decode",
    )(page_table, q.reshape(batch, 1, ATT_W),
      *([cache_k] * PAGES_PER_STEP), *([cache_v] * PAGES_PER_STEP))
    return out.reshape(batch, ATT_W)


def _block_diag(w):
    nblk, bk, bj = w.shape
    eye = jnp.eye(nblk, dtype=w.dtype)
    return jnp.einsum("nkj,nm->nkmj", w, eye).reshape(nblk * bk, nblk * bj)


def _layer_weights(l, norm_ffn1, ffn1_w1, ffn1_w3, ffn1_w2, norm_mix, w_in, b_gate, conv_w, conv_b,
                   lru_wa, lru_ba, lru_wx, lru_bx, lru_lambda, w_branch, w_out, norm_ffn2,
                   ffn2_w1, ffn2_w3, ffn2_w2):
    vec = lambda a: a[l].reshape(1, -1)
    bf = lambda a: a[l].astype(BF16)
    return {
        "ln1": vec(norm_ffn1), "f1_w1": bf(ffn1_w1), "f1_w3": bf(ffn1_w3), "f1_w2": bf(ffn1_w2),
        "ln_mix": vec(norm_mix), "w_in": bf(w_in), "b_gate": b_gate[l],
        "lru": {"conv_w": conv_w[l], "conv_b": vec(conv_b),
                "wa": _block_diag(lru_wa[l]).astype(BF16), "ba": vec(lru_ba),
                "wx": _block_diag(lru_wx[l]).astype(BF16), "bx": vec(lru_bx), "lam": vec(lru_lambda)},
        "w_branch": bf(w_branch), "w_out": bf(w_out),
        "ln2": vec(norm_ffn2), "f2_w1": bf(ffn2_w1), "f2_w3": bf(ffn2_w3), "f2_w2": bf(ffn2_w2),
    }


FFN_TILE_F = 256


def kernel(x_prompt, x_sample, cache_moba_k, cache_moba_v, cache_sb_k, cache_sb_v, state_lru_h,
           state_lru_conv, page_table, rel_bias, norm_ffn1, ffn1_w1, ffn1_w3, ffn1_w2, norm_mix, w_in,
           b_gate, conv_w, conv_b, lru_wa, lru_ba, lru_wx, lru_bx, lru_lambda, w_branch, w_out,
           norm_ffn2, ffn2_w1, ffn2_w3, ffn2_w2, norm_final):
    bp, seq, d = x_prompt.shape
    db = x_sample.shape[0]
    depth = w_in.shape[0]
    n_phys = cache_moba_k.shape[1]
    toeplitz, far, dec, own0 = _bias_tables(rel_bias)
    far_dec = jnp.broadcast_to(far[:, 0, :1], (N_HEADS, MOBA_BLOCK))
    gf = norm_final.reshape(1, d)
    paged = lambda c: c.reshape(depth, n_phys, PAGE_SIZE, ATT_W)
    cmk, cmv, csk, csv = paged(cache_moba_k), paged(cache_moba_v), paged(cache_sb_k), paged(cache_sb_v)

    yp = x_prompt.reshape(bp * seq, d)
    ys = x_sample.reshape(db, d)
    tm_p = 1024 if (bp * seq) % 1024 == 0 else MOBA_BLOCK
    zeros_buf = jnp.zeros((bp, CONV_W - 1, LRU_W), F32)
    zeros_h = jnp.zeros((bp, 1, LRU_W), F32)
    outs = {k: [] for k in ("p_mk", "p_mv", "p_sk", "p_sv", "p_h", "p_cv",
                            "s_mk", "s_mv", "s_sk", "s_sv", "s_h", "s_cv")}
    for l in range(depth):
        lw = _layer_weights(l, norm_ffn1, ffn1_w1, ffn1_w3, ffn1_w2, norm_mix, w_in, b_gate, conv_w,
                            conv_b, lru_wa, lru_ba, lru_wx, lru_bx, lru_lambda, w_branch, w_out,
                            norm_ffn2, ffn2_w1, ffn2_w3, ffn2_w2)
        last = l == depth - 1

        yp = _ffn(yp, lw["ln1"], lw["f1_w1"], lw["f1_w3"], lw["f1_w2"], gf,
                  final_norm=False, tm=tm_p, tf=FFN_TILE_F)
        qa, ka, va, xb, gb, qc, kc, vc, gl = _inproj(yp, lw["ln_mix"], lw["w_in"], tm=tm_p // 2)
        o_a = _moba_prompt(qa, ka, va, toeplitz, far, batch=bp, seq=seq)
        o_b, h_new, conv_new = _lru_prompt(xb, gb, zeros_buf, zeros_h, lw["lru"],
                                           batch=bp, seq=seq, tc=min(seq, 512))
        o_c = _sb_prompt(qc, kc, vc, batch=bp, seq=seq)
        yp = _merge(o_a, o_b, o_c, gl, yp, lw["b_gate"], lw["w_branch"], lw["w_out"], tm=MOBA_BLOCK)
        yp = _ffn(yp, lw["ln2"], lw["f2_w1"], lw["f2_w3"], lw["f2_w2"], gf,
                  final_norm=last, tm=tm_p, tf=FFN_TILE_F)
        kv_shape = (bp, seq, N_HEADS, HEAD_DIM)
        outs["p_mk"].append(ka.reshape(kv_shape)); outs["p_mv"].append(va.reshape(kv_shape))
        outs["p_sk"].append(kc.reshape(kv_shape)); outs["p_sv"].append(vc.reshape(kv_shape))
        outs["p_h"].append(h_new.reshape(bp, LRU_W)); outs["p_cv"].append(conv_new)

        ys = _ffn(ys, lw["ln1"], lw["f1_w1"], lw["f1_w3"], lw["f1_w2"], gf,
                  final_norm=False, tm=db, tf=FFN_TILE_F)
        qa, ka, va, xb, gb, qc, kc, vc, gl = _inproj(ys, lw["ln_mix"], lw["w_in"], tm=db)
        o_a = _moba_decode(page_table, qa, ka, va, cmk, cmv, dec, far_dec, own0, layer=l)
        o_b, h_new, conv_new = _lru_step(xb, gb, jnp.transpose(state_lru_conv[l], (1, 0, 2)),
                                         state_lru_h[l], lw["lru"])
        o_c = _sb_decode(page_table, qc, csk, csv, layer=l)
        ys = _merge(o_a, o_b, o_c, gl, ys, lw["b_gate"], lw["w_branch"], lw["w_out"], tm=db)
        ys = _ffn(ys, lw["ln2"], lw["f2_w1"], lw["f2_w3"], lw["f2_w2"], gf,
                  final_norm=last, tm=db, tf=FFN_TILE_F)
        kv_shape = (db, 1, N_HEADS, HEAD_DIM)
        outs["s_mk"].append(ka.reshape(kv_shape)); outs["s_mv"].append(va.reshape(kv_shape))
        outs["s_sk"].append(kc.reshape(kv_shape)); outs["s_sv"].append(vc.reshape(kv_shape))
        outs["s_h"].append(h_new); outs["s_cv"].append(jnp.transpose(conv_new, (1, 0, 2)))

    st = {k: jnp.stack(v) for k, v in outs.items()}
    return (yp.reshape(bp, seq, d), ys.reshape(db, 1, d),
            st["p_mk"], st["p_mv"], st["p_sk"], st["p_sv"], st["p_h"], st["p_cv"],
            st["s_mk"], st["s_mv"], st["s_sk"], st["s_sv"], st["s_h"], st["s_cv"])
```

```python
import functools
import math

import jax
import jax.numpy as jnp
from jax import lax
from jax.experimental import pallas as pl
from jax.experimental.pallas import tpu as pltpu

F32 = jnp.float32
BF16 = jnp.bfloat16

HEAD_DIM = 64
N_HEADS = 8
ATT_W = N_HEADS * HEAD_DIM
LANES = 128
HEADS_PER_TILE = LANES // HEAD_DIM
N_PAIRS = ATT_W // LANES
MOBA_BLOCK = 256
MOBA_TOPK = 3
PAGE_SIZE = 128
PAGES_PER_MOBA_BLOCK = MOBA_BLOCK // PAGE_SIZE
PAGES_PER_STEP = 8
LRU_W = 512
LRU_BLOCKS = 8
CONV_W = 4
RG_C = 8.0
N_BUCKETS = 32
MAX_DIST = 128
RMS_EPS = 1e-6
NEG = -1e30
SCALE = HEAD_DIM ** -0.5
VMEM_LIMIT = 56 * 1024 * 1024

_NT = (((1,), (1,)), ((), ()))


def _params(*sem):
    return pltpu.CompilerParams(dimension_semantics=sem, vmem_limit_bytes=VMEM_LIMIT)


def _rms(x, g):
    return x * lax.rsqrt(jnp.mean(x * x, axis=-1, keepdims=True) + RMS_EPS) * g


def _softplus(z):
    return jnp.maximum(z, 0.0) + jnp.log1p(jnp.exp(-jnp.abs(z)))


def _softplus_abs(z):
    return jnp.maximum(z, 0.0) + jnp.log(1.0 + jnp.exp(-jnp.abs(z)))


def _expm1(x):
    u = jnp.exp(x)
    um1 = u - 1.0
    return jnp.where(um1 == 0.0, x, jnp.where(um1 == -1.0, -1.0, um1 * x / jnp.log(u)))


def _dot_nt(a, b):
    return lax.dot_general(a.astype(BF16), b.astype(BF16), _NT, preferred_element_type=F32)


def _ffn_kernel(x_ref, g_ref, w1_ref, w3_ref, w2_ref, gf_ref, o_ref, h_ref, acc_ref, *, final_norm):
    j = pl.program_id(1)

    @pl.when(j == 0)
    def _():
        h_ref[...] = _rms(x_ref[...], g_ref[...]).astype(BF16)
        acc_ref[...] = jnp.zeros_like(acc_ref)

    h = h_ref[...]
    a = jnp.dot(h, w1_ref[...], preferred_element_type=F32)
    b = jnp.dot(h, w3_ref[...], preferred_element_type=F32)
    u = a * jax.nn.sigmoid(a) * b
    acc_ref[...] += jnp.dot(u.astype(BF16), w2_ref[...], preferred_element_type=F32)

    @pl.when(j == pl.num_programs(1) - 1)
    def _():
        y = x_ref[...] + 0.5 * acc_ref[...]
        if final_norm:
            y = _rms(y, gf_ref[...])
        o_ref[...] = y


def _ffn(x, g, w1, w3, w2, gf, *, final_norm, tm, tf):
    n, d = x.shape
    dff = w1.shape[1]
    return pl.pallas_call(
        functools.partial(_ffn_kernel, final_norm=final_norm),
        grid=(n // tm, dff // tf),
        in_specs=[
            pl.BlockSpec((tm, d), lambda i, j: (i, 0)),
            pl.BlockSpec((1, d), lambda i, j: (0, 0)),
            pl.BlockSpec((d, tf), lambda i, j: (0, j)),
            pl.BlockSpec((d, tf), lambda i, j: (0, j)),
            pl.BlockSpec((tf, d), lambda i, j: (j, 0)),
            pl.BlockSpec((1, d), lambda i, j: (0, 0)),
        ],
        out_specs=pl.BlockSpec((tm, d), lambda i, j: (i, 0)),
        out_shape=jax.ShapeDtypeStruct((n, d), F32),
        scratch_shapes=[pltpu.VMEM((tm, d), BF16), pltpu.VMEM((tm, d), F32)],
        compiler_params=_params("parallel", "arbitrary"),
        name="ffn",
    )(x, g, w1, w3, w2, gf)


N_IN_SPLIT = 8


def _inproj_kernel(x_ref, g_ref, w_ref, *refs):
    outs, h_ref = refs[:-1], refs[-1]
    j = pl.program_id(1)

    @pl.when(j == 0)
    def _():
        h_ref[...] = _rms(x_ref[...], g_ref[...]).astype(BF16)

    p = jnp.dot(h_ref[...], w_ref[...], preferred_element_type=F32)
    for k in range(N_IN_SPLIT // 2):
        @pl.when(j == k)
        def _(k=k):
            outs[2 * k][...] = p[:, :ATT_W]
            outs[2 * k + 1][...] = p[:, ATT_W:]

    @pl.when(j >= N_IN_SPLIT // 2)
    def _():
        outs[N_IN_SPLIT][...] = p


def _inproj(x, g, w_in, *, tm):
    n, d = x.shape
    cols = w_in.shape[1]
    tn = 2 * ATT_W
    n_split_tiles = N_IN_SPLIT // 2
    n_gate_tiles = cols // tn - n_split_tiles
    out_specs = [pl.BlockSpec((tm, ATT_W), lambda i, j: (i, 0)) for _ in range(N_IN_SPLIT)]
    out_specs.append(pl.BlockSpec((tm, tn), lambda i, j: (i, jnp.maximum(j - n_split_tiles, 0))))
    out_shape = [jax.ShapeDtypeStruct((n, ATT_W), F32) for _ in range(N_IN_SPLIT)]
    out_shape.append(jax.ShapeDtypeStruct((n, n_gate_tiles * tn), F32))
    return pl.pallas_call(
        _inproj_kernel,
        grid=(n // tm, cols // tn),
        in_specs=[
            pl.BlockSpec((tm, d), lambda i, j: (i, 0)),
            pl.BlockSpec((1, d), lambda i, j: (0, 0)),
            pl.BlockSpec((d, tn), lambda i, j: (0, j)),
        ],
        out_specs=out_specs,
        out_shape=out_shape,
        scratch_shapes=[pltpu.VMEM((tm, d), BF16)],
        compiler_params=_params("parallel", "arbitrary"),
        name="inproj",
    )(x, g, w_in)


def _merge_kernel(oa_ref, ob_ref, oc_ref, gl_ref, x_ref, bg_ref, wb_ref, wo_ref, o_ref):
    d = x_ref.shape[1]
    merged = None
    for n, o_n in enumerate((oa_ref, ob_ref, oc_ref)):
        proj = jnp.dot(o_n[...].astype(BF16), wb_ref[n], preferred_element_type=F32)
        gate = jax.nn.sigmoid(gl_ref[:, n * d:(n + 1) * d] + bg_ref[n:n + 1, :])
        merged = gate * proj if merged is None else merged + gate * proj
    o_ref[...] = x_ref[...] + jnp.dot(merged.astype(BF16), wo_ref[...], preferred_element_type=F32)


def _merge(oa, ob, oc, gl, x, bg, wb, wo, *, tm):
    n, d = x.shape
    row = lambda w: pl.BlockSpec((tm, w), lambda i: (i, 0))
    return pl.pallas_call(
        _merge_kernel,
        grid=(n // tm,),
        in_specs=[row(ATT_W), row(ATT_W), row(ATT_W), row(gl.shape[1]), row(d),
                  pl.BlockSpec(bg.shape, lambda i: (0, 0)),
                  pl.BlockSpec(wb.shape, lambda i: (0, 0, 0)),
                  pl.BlockSpec(wo.shape, lambda i: (0, 0))],
        out_specs=row(d),
        out_shape=jax.ShapeDtypeStruct((n, d), F32),
        compiler_params=_params("parallel"),
        name="merge",
    )(oa, ob, oc, gl, x, bg, wb, wo)


def _rel_bucket(dist):
    max_exact = N_BUCKETS // 2
    d = jnp.maximum(dist, 0)
    df = jnp.maximum(d, 1).astype(F32)
    large = max_exact + (jnp.log(df / max_exact) / math.log(MAX_DIST / max_exact)
                         * (N_BUCKETS - max_exact)).astype(jnp.int32)
    large = jnp.minimum(large, N_BUCKETS - 1)
    return jnp.where(d < max_exact, d, large)


def _bias_tables(rel_bias):
    period = 4 * MOBA_BLOCK
    k = jnp.arange(period, dtype=jnp.int32)
    k = jnp.where(k < period // 2, k, k - period)
    u = rel_bias[_rel_bucket(jnp.maximum(MOBA_BLOCK + k, 0))].T
    n_keys = 2 * MOBA_BLOCK
    toeplitz = jnp.tile(u, (1, n_keys))[:, :n_keys * (period - 1)]
    toeplitz = toeplitz.reshape(N_HEADS, n_keys, period - 1)[:, :, :MOBA_BLOCK]
    toeplitz = toeplitz.reshape(N_PAIRS, HEADS_PER_TILE, n_keys, MOBA_BLOCK)
    toeplitz = jnp.transpose(toeplitz, (0, 2, 1, 3)).reshape(N_PAIRS, n_keys, HEADS_PER_TILE * MOBA_BLOCK)
    far = rel_bias[_rel_bucket(jnp.int32(2 * MOBA_BLOCK))]
    far_pair = jnp.repeat(far.reshape(N_PAIRS, 1, HEADS_PER_TILE), MOBA_BLOCK, axis=2)
    dec = rel_bias[_rel_bucket(MOBA_BLOCK - jnp.arange(MOBA_BLOCK, dtype=jnp.int32))].T
    own0 = rel_bias[_rel_bucket(jnp.int32(0))][:, None]
    return toeplitz, far_pair, dec, far[:, None], own0


PAIR_Q = HEADS_PER_TILE * MOBA_BLOCK


def _stacked_queries(q):
    lane = lax.broadcasted_iota(jnp.int32, (1, LANES), 1)
    qs = q * SCALE
    return jnp.concatenate([jnp.where((lane // HEAD_DIM) == hh, qs, 0.0).astype(BF16)
                            for hh in range(HEADS_PER_TILE)], axis=0)


def _pair_corners(acc):
    return jnp.concatenate([acc[hh * HEAD_DIM:(hh + 1) * HEAD_DIM, hh * MOBA_BLOCK:(hh + 1) * MOBA_BLOCK]
                            for hh in range(HEADS_PER_TILE)], axis=0)


def _moba_prompt_kernel(q_ref, kf_ref, kb_ref, vt_ref, tb_ref, far_ref, o_ref, kmean_ref, sel_ref):
    i = pl.program_id(2)
    nb = kb_ref.shape[0] // MOBA_BLOCK

    @pl.when(i == 0)
    def _():
        for n in range(nb):
            kmean_ref[n:n + 1, :] = jnp.mean(kf_ref[n * MOBA_BLOCK:(n + 1) * MOBA_BLOCK, :],
                                             axis=0, keepdims=True)

    q = q_ref[...]
    qcat = _stacked_queries(q)
    lane = lax.broadcasted_iota(jnp.int32, (1, LANES), 1)
    blk_row = lax.broadcasted_iota(jnp.int32, (nb, 1), 0)

    gate = jnp.concatenate(
        [lax.dot_general(jnp.where((lane // HEAD_DIM) == hh, kmean_ref[...], 0.0), q, _NT,
                         precision=lax.Precision.HIGHEST, preferred_element_type=F32)
         for hh in range(HEADS_PER_TILE)], axis=1)
    gate = jnp.where(blk_row < i, gate, NEG)
    rank = jnp.zeros_like(gate)
    for m in range(nb):
        gm = gate[m:m + 1, :]
        ahead = (gm > gate) | ((gm == gate) & (blk_row > m))
        rank = rank + ahead.astype(F32)
    sel_ref[...] = ((rank < MOBA_TOPK) & (blk_row < i)).astype(F32)

    def block(n, bias, keep, m_run, l_run, acc):
        start = pl.multiple_of(n * MOBA_BLOCK, MOBA_BLOCK)
        s = _dot_nt(kb_ref[pl.ds(start, MOBA_BLOCK), :], qcat) + bias
        s = jnp.where(keep, s, NEG)
        m_new = jnp.max(s, axis=0, keepdims=True)
        if m_run is not None:
            m_new = jnp.maximum(m_run, m_new)
        p = jnp.exp(s - m_new)
        l_new = jnp.sum(p, axis=0, keepdims=True)
        pv = jnp.dot(vt_ref[n], p.astype(BF16), preferred_element_type=F32)
        if m_run is None:
            return m_new, l_new, pv
        alpha = jnp.exp(m_run - m_new)
        return m_new, alpha * l_run + l_new, alpha * acc + pv

    key_i = lax.broadcasted_iota(jnp.int32, (MOBA_BLOCK, PAIR_Q), 0)
    qry_i = lax.broadcasted_iota(jnp.int32, (MOBA_BLOCK, PAIR_Q), 1) % MOBA_BLOCK
    state0 = block(i, tb_ref[MOBA_BLOCK:, :], key_i <= qry_i, None, None, None)

    def body(n, state):
        bias = jnp.where(n == i - 1, tb_ref[:MOBA_BLOCK, :], far_ref[...])
        return block(n, bias, sel_ref[pl.ds(n, 1), :] > 0.5, *state)

    _, l_fin, acc_fin = lax.fori_loop(0, i, body, state0)
    o_ref[...] = _pair_corners(acc_fin / l_fin)


def _moba_prompt(q, k_f32, k_bf, vt, toeplitz, far, *, batch, seq):
    nb = seq // MOBA_BLOCK
    return pl.pallas_call(
        _moba_prompt_kernel,
        grid=(batch, N_PAIRS, nb),
        in_specs=[
            pl.BlockSpec((MOBA_BLOCK, LANES), lambda b, p, i: (b * nb + i, p)),
            pl.BlockSpec((seq, LANES), lambda b, p, i: (b, p)),
            pl.BlockSpec((seq, LANES), lambda b, p, i: (b, p)),
            pl.BlockSpec((nb, LANES, MOBA_BLOCK), lambda b, p, i: (b, p, 0)),
            pl.BlockSpec((None, 2 * MOBA_BLOCK, PAIR_Q), lambda b, p, i: (p, 0, 0)),
            pl.BlockSpec((None, 1, PAIR_Q), lambda b, p, i: (p, 0, 0)),
        ],
        out_specs=pl.BlockSpec((None, LANES, MOBA_BLOCK), lambda b, p, i: (b * nb + i, p, 0)),
        out_shape=jax.ShapeDtypeStruct((batch * nb, ATT_W, MOBA_BLOCK), F32),
        scratch_shapes=[pltpu.VMEM((nb, LANES), F32), pltpu.VMEM((nb, PAIR_Q), F32)],
        compiler_params=_params("parallel", "parallel", "arbitrary"),
        name="moba_prompt",
    )(q, k_f32, k_bf, vt, toeplitz, far)


SB_BLOCK = 256


def _sb_prompt_kernel(q_ref, kb_ref, vt_ref, o_ref):
    i = pl.program_id(2)
    qcat = _stacked_queries(q_ref[...])
    row_i = lax.broadcasted_iota(jnp.int32, (SB_BLOCK, SB_BLOCK), 0)
    col_i = lax.broadcasted_iota(jnp.int32, (SB_BLOCK, SB_BLOCK), 1)
    later = (row_i < col_i).astype(BF16)

    def block(n, mask, carry):
        start = pl.multiple_of(n * SB_BLOCK, SB_BLOCK)
        z = _dot_nt(kb_ref[pl.ds(start, SB_BLOCK), :], qcat)
        sp = _softplus_abs(z)
        log_keep = -sp if mask is None else jnp.where(mask, -sp, 0.0)
        hi = log_keep.astype(BF16)
        lo = (log_keep - hi.astype(F32)).astype(BF16)
        both = jnp.dot(later, jnp.concatenate([hi, lo], axis=1), preferred_element_type=F32)
        between = both[:, :PAIR_Q] + both[:, PAIR_Q:]
        if carry is not None:
            between = between + carry
        w = jnp.exp(z - sp + between)
        if mask is not None:
            w = jnp.where(mask, w, 0.0)
        pv = jnp.dot(vt_ref[n], w.astype(BF16), preferred_element_type=F32)
        return jnp.sum(log_keep, axis=0, keepdims=True), pv

    key_i = lax.broadcasted_iota(jnp.int32, (SB_BLOCK, PAIR_Q), 0)
    qry_i = lax.broadcasted_iota(jnp.int32, (SB_BLOCK, PAIR_Q), 1) % SB_BLOCK
    state0 = block(i, key_i < qry_i, None)

    def body(t, state):
        carry, acc = state
        kept, pv = block(i - 1 - t, None, carry)
        return carry + kept, acc + pv

    _, acc_fin = lax.fori_loop(0, i, body, state0)
    o_ref[...] = _pair_corners(acc_fin)


def _sb_prompt(q, k_bf, vt, *, batch, seq):
    nb = seq // SB_BLOCK
    return pl.pallas_call(
        _sb_prompt_kernel,
        grid=(batch, N_PAIRS, nb),
        in_specs=[
            pl.BlockSpec((SB_BLOCK, LANES), lambda b, p, i: (b * nb + i, p)),
            pl.BlockSpec((seq, LANES), lambda b, p, i: (b, p)),
            pl.BlockSpec((nb, LANES, SB_BLOCK), lambda b, p, i: (b, p, 0)),
        ],
        out_specs=pl.BlockSpec((None, LANES, SB_BLOCK), lambda b, p, i: (b * nb + i, p, 0)),
        out_shape=jax.ShapeDtypeStruct((batch * nb, ATT_W, SB_BLOCK), F32),
        compiler_params=_params("parallel", "parallel", "arbitrary"),
        name="sb_prompt",
    )(q, k_bf, vt)


def _lru_gates(xc, wa_ref, ba_ref, wx_ref, bx_ref, lam_ref):
    xc_bf = xc.astype(BF16)
    r = jax.nn.sigmoid(jnp.dot(xc_bf, wa_ref[...], preferred_element_type=F32) + ba_ref[...])
    gi = jax.nn.sigmoid(jnp.dot(xc_bf, wx_ref[...], preferred_element_type=F32) + bx_ref[...])
    log_a = -RG_C * r * _softplus(-lam_ref[...])
    a = jnp.exp(log_a)
    b = jnp.sqrt(-_expm1(2.0 * log_a)) * (gi * xc)
    return a, b


def _lru_prompt_kernel(x_ref, g_ref, buf_ref, h0_ref, cw_ref, cb_ref, wa_ref, ba_ref, wx_ref, bx_ref,
                       lam_ref, y_ref, hlast_ref, conv_ref, tail_ref, hc_ref):
    t = pl.program_id(1)
    tc = x_ref.shape[0]
    tail_rows = tail_ref.shape[0]

    @pl.when(t == 0)
    def _():
        tail_ref[...] = jnp.zeros_like(tail_ref)
        tail_ref[tail_rows - (CONV_W - 1):, :] = buf_ref[...]
        hc_ref[...] = h0_ref[...]

    x = x_ref[...]
    xe = jnp.concatenate([tail_ref[...], x], axis=0)
    xc = cb_ref[...] + cw_ref[CONV_W - 1:CONV_W, :] * x
    for k in range(1, CONV_W):
        xc = xc + cw_ref[CONV_W - 1 - k:CONV_W - k, :] * pltpu.roll(xe, k, 0)[tail_rows:, :]
    a, b = _lru_gates(xc, wa_ref, ba_ref, wx_ref, bx_ref, lam_ref)

    row = lax.broadcasted_iota(jnp.int32, (tc, 1), 0)
    k = 1
    while k < tc:
        a_sh = jnp.where(row >= k, pltpu.roll(a, k, 0), 1.0)
        b_sh = jnp.where(row >= k, pltpu.roll(b, k, 0), 0.0)
        b = a * b_sh + b
        a = a * a_sh
        k *= 2
    h = a * hc_ref[...] + b
    y_ref[...] = h * jax.nn.gelu(g_ref[...], approximate=True)
    hc_ref[...] = h[tc - 1:tc, :]
    tail_ref[...] = x[tc - tail_rows:, :]

    @pl.when(t == pl.num_programs(1) - 1)
    def _():
        hlast_ref[...] = h[tc - 1:tc, :]
        conv_ref[...] = x[tc - (CONV_W - 1):, :]


def _lru_prompt(x, g, buf, h0, lw, *, batch, seq, tc):
    nt = seq // tc
    w = x.shape[1]
    vec = pl.BlockSpec((1, w), lambda b, t: (0, 0))
    mat = pl.BlockSpec((w, w), lambda b, t: (0, 0))
    return pl.pallas_call(
        _lru_prompt_kernel,
        grid=(batch, nt),
        in_specs=[
            pl.BlockSpec((tc, w), lambda b, t: (b * nt + t, 0)),
            pl.BlockSpec((tc, w), lambda b, t: (b * nt + t, 0)),
            pl.BlockSpec((None, CONV_W - 1, w), lambda b, t: (b, 0, 0)),
            pl.BlockSpec((None, 1, w), lambda b, t: (b, 0, 0)),
            pl.BlockSpec((CONV_W, w), lambda b, t: (0, 0)),
            vec, mat, vec, mat, vec, vec,
        ],
        out_specs=[
            pl.BlockSpec((tc, w), lambda b, t: (b * nt + t, 0)),
            pl.BlockSpec((None, 1, w), lambda b, t: (b, 0, 0)),
            pl.BlockSpec((None, CONV_W - 1, w), lambda b, t: (b, 0, 0)),
        ],
        out_shape=[jax.ShapeDtypeStruct(x.shape, F32),
                   jax.ShapeDtypeStruct((batch, 1, w), F32),
                   jax.ShapeDtypeStruct((batch, CONV_W - 1, w), F32)],
        scratch_shapes=[pltpu.VMEM((8, w), F32), pltpu.VMEM((1, w), F32)],
        compiler_params=_params("parallel", "arbitrary"),
        name="lru_prompt",
    )(x, g, buf, h0, lw["conv_w"], lw["conv_b"], lw["wa"], lw["ba"], lw["wx"], lw["bx"], lw["lam"])


def _lru_step_kernel(x_ref, g_ref, buf_ref, h0_ref, cw_ref, cb_ref, wa_ref, ba_ref, wx_ref, bx_ref,
                     lam_ref, y_ref, h_ref, conv_ref):
    x = x_ref[...]
    xc = cb_ref[...] + cw_ref[CONV_W - 1:CONV_W, :] * x
    for k in range(CONV_W - 1):
        xc = xc + cw_ref[k:k + 1, :] * buf_ref[k]
    a, b = _lru_gates(xc, wa_ref, ba_ref, wx_ref, bx_ref, lam_ref)
    h = a * h0_ref[...] + b
    y_ref[...] = h * jax.nn.gelu(g_ref[...], approximate=True)
    h_ref[...] = h
    for k in range(CONV_W - 2):
        conv_ref[k] = buf_ref[k + 1]
    conv_ref[CONV_W - 2] = x


def _lru_step(x, g, buf, h0, lw):
    n, w = x.shape
    return pl.pallas_call(
        _lru_step_kernel,
        out_shape=[jax.ShapeDtypeStruct((n, w), F32), jax.ShapeDtypeStruct((n, w), F32),
                   jax.ShapeDtypeStruct((CONV_W - 1, n, w), F32)],
        compiler_params=pltpu.CompilerParams(vmem_limit_bytes=VMEM_LIMIT),
        name="lru_step",
    )(x, g, buf, h0, lw["conv_w"], lw["conv_b"], lw["wa"], lw["ba"], lw["wx"], lw["bx"], lw["lam"])


def _page_scores(qb, kt_ref):
    return jnp.concatenate([jnp.sum(qb[h] * kt_ref[h], axis=0, keepdims=True)
                            for h in range(N_HEADS)], axis=0)


def _weighted_values(w, vt_ref):
    return jnp.stack([jnp.broadcast_to(w[h:h + 1, :], (HEAD_DIM, PAGE_SIZE)) * vt_ref[h]
                      for h in range(N_HEADS)], axis=0)


def _sum_positions(acc):
    ones = jnp.ones((N_HEADS, PAGE_SIZE), F32)
    rows = [lax.dot_general(ones, acc[h], _NT, precision=lax.Precision.HIGHEST,
                            preferred_element_type=F32)[:1] for h in range(N_HEADS)]
    return jnp.concatenate(rows, axis=0)


def _moba_decode_kernel(pt_ref, q_ref, q2_ref, kn_ref, vn_ref, dec_ref, far_ref, own_ref, *refs):
    del pt_ref
    n_pg = PAGES_PER_STEP
    k_pages, v_pages = refs[:n_pg], refs[n_pg:2 * n_pg]
    o_ref, gate_ref, m_ref, l_ref, acc_ref = refs[2 * n_pg:]
    c = pl.program_id(1)
    n_steps = pl.num_programs(1)
    blocks_per_step = n_pg // PAGES_PER_MOBA_BLOCK
    qb = jnp.broadcast_to(q_ref[...] * SCALE, (N_HEADS, HEAD_DIM, PAGE_SIZE))
    lane = lax.broadcasted_iota(jnp.int32, (1, LANES), 1)

    @pl.when(c == 0)
    def _():
        gate_ref[...] = jnp.full_like(gate_ref, NEG)
        m_ref[...] = jnp.full_like(m_ref, NEG)
        l_ref[...] = jnp.zeros_like(l_ref)

    for j in range(blocks_per_step):
        n = c * blocks_per_step + j
        pages = range(PAGES_PER_MOBA_BLOCK * j, PAGES_PER_MOBA_BLOCK * (j + 1))
        z = [_page_scores(qb, k_pages[pg]) for pg in pages]
        g = sum(jnp.sum(zt, axis=-1, keepdims=True) for zt in z) * (1.0 / (SCALE * MOBA_BLOCK))
        newest = n == n_steps * blocks_per_step - 1
        s = [zt + jnp.where(newest, dec_ref[:, t * PAGE_SIZE:(t + 1) * PAGE_SIZE], far_ref[...])
             for t, zt in enumerate(z)]
        m = functools.reduce(jnp.maximum, [jnp.max(st, axis=-1, keepdims=True) for st in s])
        p = [jnp.exp(st - m) for st in s]
        l = sum(jnp.sum(pt, axis=-1, keepdims=True) for pt in p)
        acc_ref[n] = sum(_weighted_values(pt, v_pages[pg]) for pt, pg in zip(p, pages))
        gate_ref[...] = jnp.where(lane == n, g, gate_ref[...])
        m_ref[...] = jnp.where(lane == n, m, m_ref[...])
        l_ref[...] = jnp.where(lane == n, l, l_ref[...])

    @pl.when(c == n_steps - 1)
    def _():
        nb = n_steps * blocks_per_step
        gate = gate_ref[...]
        rank = jnp.zeros_like(gate)
        for mm in range(nb):
            gm = gate[:, mm:mm + 1]
            ahead = (gm > gate) | ((gm == gate) & (lane > mm))
            rank = rank + ahead.astype(F32)
        sel = (rank < MOBA_TOPK) & (lane < nb)
        s_own = jnp.sum(q2_ref[...] * SCALE * kn_ref[...], axis=-1, keepdims=True) + own_ref[...]
        m_sel = jnp.where(sel, m_ref[...], NEG)
        m_tot = jnp.maximum(jnp.max(m_sel, axis=-1, keepdims=True), s_own)
        wgt = jnp.where(sel, jnp.exp(m_sel - m_tot), 0.0)
        w_own = jnp.exp(s_own - m_tot)
        l_tot = jnp.sum(wgt * l_ref[...], axis=-1, keepdims=True) + w_own
        merged = []
        for h in range(N_HEADS):
            tot = jnp.zeros((HEAD_DIM, PAGE_SIZE), F32)
            for mm in range(nb):
                tot = tot + jnp.broadcast_to(wgt[h:h + 1, mm:mm + 1], (HEAD_DIM, PAGE_SIZE)) * acc_ref[mm, h]
            merged.append(tot)
        o = w_own * vn_ref[...] + _sum_positions(merged)
        o_ref[...] = o / l_tot


def _page_spec(layer, slot, order):
    def index_map(b, c, pt):
        n_pages = pt.shape[1]
        pos = c * PAGES_PER_STEP + slot
        return (layer, pt[b, pos if order > 0 else n_pages - 1 - pos], 0, 0, 0)
    return pl.BlockSpec((None, None, N_HEADS, HEAD_DIM, PAGE_SIZE), index_map)


def _moba_decode(page_table, q, k_new, v_new, cache_k, cache_v, dec, far, own0, *, layer):
    batch, n_pages = page_table.shape
    n_steps = n_pages // PAGES_PER_STEP
    nb = n_pages // PAGES_PER_MOBA_BLOCK
    heads = pl.BlockSpec((None, N_HEADS, HEAD_DIM), lambda b, c, pt: (b, 0, 0))
    column = pl.BlockSpec((None, N_HEADS, HEAD_DIM, 1), lambda b, c, pt: (b, 0, 0, 0))
    const = lambda a: pl.BlockSpec(a.shape, lambda b, c, pt: (0,) * a.ndim)
    grid_spec = pltpu.PrefetchScalarGridSpec(
        num_scalar_prefetch=1,
        grid=(batch, n_steps),
        in_specs=[column, heads, heads, heads, const(dec), const(far), const(own0)]
        + [_page_spec(layer, s, +1) for s in range(PAGES_PER_STEP)]
        + [_page_spec(layer, s, +1) for s in range(PAGES_PER_STEP)],
        out_specs=heads,
        scratch_shapes=[pltpu.VMEM((N_HEADS, LANES), F32), pltpu.VMEM((N_HEADS, LANES), F32),
                        pltpu.VMEM((N_HEADS, LANES), F32),
                        pltpu.VMEM((nb, N_HEADS, HEAD_DIM, PAGE_SIZE), F32)],
    )
    return pl.pallas_call(
        _moba_decode_kernel,
        grid_spec=grid_spec,
        out_shape=jax.ShapeDtypeStruct((batch, N_HEADS, HEAD_DIM), F32),
        compiler_params=_params("parallel", "arbitrary"),
        name="moba_decode",
    )(page_table, q[..., None], q, k_new, v_new, dec, far, own0,
      *([cache_k] * PAGES_PER_STEP), *([cache_v] * PAGES_PER_STEP))


def _sb_decode_kernel(pt_ref, q_ref, *refs):
    del pt_ref
    n_pg = PAGES_PER_STEP
    k_pages, v_pages = refs[:n_pg], refs[n_pg:2 * n_pg]
    o_ref, carry_ref, acc_ref = refs[2 * n_pg:]
    c = pl.program_id(1)
    qb = jnp.broadcast_to(q_ref[...] * SCALE, (N_HEADS, HEAD_DIM, PAGE_SIZE))
    lane = lax.broadcasted_iota(jnp.int32, (1, PAGE_SIZE), 1)

    @pl.when(c == 0)
    def _():
        carry_ref[...] = jnp.zeros_like(carry_ref)
        acc_ref[...] = jnp.zeros_like(acc_ref)

    carry = carry_ref[...]
    for j in range(n_pg):
        z = _page_scores(qb, k_pages[j])
        sp = _softplus_abs(z)
        log_keep = -sp
        incl = log_keep
        shift = 1
        while shift < PAGE_SIZE:
            moved = pltpu.roll(incl, PAGE_SIZE - shift, 1)
            incl = incl + jnp.where(lane < PAGE_SIZE - shift, moved, 0.0)
            shift *= 2
        w = jnp.exp(z - sp + incl - log_keep + carry)
        acc_ref[...] += _weighted_values(w, v_pages[j])
        carry = carry + jnp.sum(log_keep, axis=-1, keepdims=True)
    carry_ref[...] = carry

    @pl.when(c == pl.num_programs(1) - 1)
    def _():
        o_ref[...] = _sum_positions(acc_ref[...])


def _sb_decode(page_table, q, cache_k, cache_v, *, layer):
    batch, n_pages = page_table.shape
    n_steps = n_pages // PAGES_PER_STEP
    grid_spec = pltpu.PrefetchScalarGridSpec(
        num_scalar_prefetch=1,
        grid=(batch, n_steps),
        in_specs=[pl.BlockSpec((None, N_HEADS, HEAD_DIM, 1), lambda b, c, pt: (b, 0, 0, 0))]
        + [_page_spec(layer, s, -1) for s in range(PAGES_PER_STEP)]
        + [_page_spec(layer, s, -1) for s in range(PAGES_PER_STEP)],
        out_specs=pl.BlockSpec((None, N_HEADS, HEAD_DIM), lambda b, c, pt: (b, 0, 0)),
        scratch_shapes=[pltpu.VMEM((N_HEADS, 1), F32),
                        pltpu.VMEM((N_HEADS, HEAD_DIM, PAGE_SIZE), F32)],
    )
    return pl.pallas_call(
        _sb_decode_kernel,
        grid_spec=grid_spec,
        out_shape=jax.ShapeDtypeStruct((batch, N_HEADS, HEAD_DIM), F32),
        compiler_params=_params("parallel", "arbitrary"),
        name="sb_decode",
    )(page_table, q[..., None], *([cache_k] * PAGES_PER_STEP), *([cache_v] * PAGES_PER_STEP))


def _block_diag(w):
    nblk, bk, bj = w.shape
    eye = jnp.eye(nblk, dtype=w.dtype)
    return jnp.einsum("nkj,nm->nkmj", w, eye).reshape(nblk * bk, nblk * bj)


def _layer_weights(l, norm_ffn1, ffn1_w1, ffn1_w3, ffn1_w2, norm_mix, w_in, b_gate, conv_w, conv_b,
                   lru_wa, lru_ba, lru_wx, lru_bx, lru_lambda, w_branch, w_out, norm_ffn2,
                   ffn2_w1, ffn2_w3, ffn2_w2):
    vec = lambda a: a[l].reshape(1, -1)
    bf = lambda a: a[l].astype(BF16)
    return {
        "ln1": vec(norm_ffn1), "f1_w1": bf(ffn1_w1), "f1_w3": bf(ffn1_w3), "f1_w2": bf(ffn1_w2),
        "ln_mix": vec(norm_mix), "w_in": bf(w_in), "b_gate": b_gate[l],
        "lru": {"conv_w": conv_w[l], "conv_b": vec(conv_b),
                "wa": _block_diag(lru_wa[l]).astype(BF16), "ba": vec(lru_ba),
                "wx": _block_diag(lru_wx[l]).astype(BF16), "bx": vec(lru_bx), "lam": vec(lru_lambda)},
        "w_branch": bf(w_branch), "w_out": bf(w_out),
        "ln2": vec(norm_ffn2), "f2_w1": bf(ffn2_w1), "f2_w3": bf(ffn2_w3), "f2_w2": bf(ffn2_w2),
    }


FFN_TILE_F = 256


def kernel(x_prompt, x_sample, cache_moba_k, cache_moba_v, cache_sb_k, cache_sb_v, state_lru_h,
           state_lru_conv, page_table, rel_bias, norm_ffn1, ffn1_w1, ffn1_w3, ffn1_w2, norm_mix, w_in,
           b_gate, conv_w, conv_b, lru_wa, lru_ba, lru_wx, lru_bx, lru_lambda, w_branch, w_out,
           norm_ffn2, ffn2_w1, ffn2_w3, ffn2_w2, norm_final):
    bp, seq, d = x_prompt.shape
    db = x_sample.shape[0]
    depth = w_in.shape[0]
    n_phys = cache_moba_k.shape[1]
    toeplitz, far, dec, far_dec, own0 = _bias_tables(rel_bias)
    gf = norm_final.reshape(1, d)
    paged = lambda c: jnp.transpose(c, (0, 1, 3, 4, 2))
    cmk, cmv, csk, csv = paged(cache_moba_k), paged(cache_moba_v), paged(cache_sb_k), paged(cache_sb_v)

    yp = x_prompt.reshape(bp * seq, d)
    ys = x_sample.reshape(db, d)
    tm_p = 1024 if (bp * seq) % 1024 == 0 else MOBA_BLOCK
    n_blk = bp * seq // MOBA_BLOCK
    transposed_blocks = lambda a: jnp.transpose(
        a.astype(BF16).reshape(n_blk, MOBA_BLOCK, ATT_W), (0, 2, 1))
    untransposed = lambda a: jnp.transpose(a, (0, 2, 1)).reshape(bp * seq, ATT_W)
    zeros_buf = jnp.zeros((bp, CONV_W - 1, LRU_W), F32)
    zeros_h = jnp.zeros((bp, 1, LRU_W), F32)
    outs = {k: [] for k in ("p_mk", "p_mv", "p_sk", "p_sv", "p_h", "p_cv",
                            "s_mk", "s_mv", "s_sk", "s_sv", "s_h", "s_cv")}
    for l in range(depth):
        lw = _layer_weights(l, norm_ffn1, ffn1_w1, ffn1_w3, ffn1_w2, norm_mix, w_in, b_gate, conv_w,
                            conv_b, lru_wa, lru_ba, lru_wx, lru_bx, lru_lambda, w_branch, w_out,
                            norm_ffn2, ffn2_w1, ffn2_w3, ffn2_w2)
        last = l == depth - 1

        yp = _ffn(yp, lw["ln1"], lw["f1_w1"], lw["f1_w3"], lw["f1_w2"], gf,
                  final_norm=False, tm=tm_p, tf=FFN_TILE_F)
        qa, ka, va, xb, gb, qc, kc, vc, gl = _inproj(yp, lw["ln_mix"], lw["w_in"], tm=tm_p // 2)
        o_a = untransposed(_moba_prompt(qa, ka, ka.astype(BF16), transposed_blocks(va), toeplitz, far,
                                        batch=bp, seq=seq))
        o_b, h_new, conv_new = _lru_prompt(xb, gb, zeros_buf, zeros_h, lw["lru"],
                                           batch=bp, seq=seq, tc=min(seq, 512))
        o_c = untransposed(_sb_prompt(qc, kc.astype(BF16), transposed_blocks(vc), batch=bp, seq=seq))
        yp = _merge(o_a, o_b, o_c, gl, yp, lw["b_gate"], lw["w_branch"], lw["w_out"], tm=MOBA_BLOCK)
        yp = _ffn(yp, lw["ln2"], lw["f2_w1"], lw["f2_w3"], lw["f2_w2"], gf,
                  final_norm=last, tm=tm_p, tf=FFN_TILE_F)
        kv_shape = (bp, seq, N_HEADS, HEAD_DIM)
        outs["p_mk"].append(ka.reshape(kv_shape)); outs["p_mv"].append(va.reshape(kv_shape))
        outs["p_sk"].append(kc.reshape(kv_shape)); outs["p_sv"].append(vc.reshape(kv_shape))
        outs["p_h"].append(h_new.reshape(bp, LRU_W)); outs["p_cv"].append(conv_new)

        ys = _ffn(ys, lw["ln1"], lw["f1_w1"], lw["f1_w3"], lw["f1_w2"], gf,
                  final_norm=False, tm=db, tf=FFN_TILE_F)
        qa, ka, va, xb, gb, qc, kc, vc, gl = _inproj(ys, lw["ln_mix"], lw["w_in"], tm=db)
        by_head = lambda a: a.reshape(db, N_HEADS, HEAD_DIM)
        o_a = _moba_decode(page_table, by_head(qa), by_head(ka), by_head(va), cmk, cmv,
                           dec, far_dec, own0, layer=l).reshape(db, ATT_W)
        o_b, h_new, conv_new = _lru_step(xb, gb, jnp.transpose(state_lru_conv[l], (1, 0, 2)),
                                         state_lru_h[l], lw["lru"])
        o_c = _sb_decode(page_table, by_head(qc), csk, csv, layer=l).reshape(db, ATT_W)
        ys = _merge(o_a, o_b, o_c, gl, ys, lw["b_gate"], lw["w_branch"], lw["w_out"], tm=db)
        ys = _ffn(ys, lw["ln2"], lw["f2_w1"], lw["f2_w3"], lw["f2_w2"], gf,
                  final_norm=last, tm=db, tf=FFN_TILE_F)
        kv_shape = (db, 1, N_HEADS, HEAD_DIM)
        outs["s_mk"].append(ka.reshape(kv_shape)); outs["s_mv"].append(va.reshape(kv_shape))
        outs["s_sk"].append(kc.reshape(kv_shape)); outs["s_sv"].append(vc.reshape(kv_shape))
        outs["s_h"].append(h_new); outs["s_cv"].append(jnp.transpose(conv_new, (1, 0, 2)))

    st = {k: jnp.stack(v) for k, v in outs.items()}
    return (yp.reshape(bp, seq, d), ys.reshape(db, 1, d),
            st["p_mk"], st["p_mv"], st["p_sk"], st["p_sv"], st["p_h"], st["p_cv"],
            st["s_mk"], st["s_mv"], st["s_sk"], st["s_sv"], st["s_h"], st["s_cv"])
```

```python
import functools
import math

import jax
import jax.numpy as jnp
from jax import lax
from jax.experimental import pallas as pl
from jax.experimental.pallas import tpu as pltpu

F32 = jnp.float32
BF16 = jnp.bfloat16

HEAD_DIM = 64
N_HEADS = 8
ATT_W = N_HEADS * HEAD_DIM
LANES = 128
HEADS_PER_TILE = LANES // HEAD_DIM
N_PAIRS = ATT_W // LANES
MOBA_BLOCK = 256
MOBA_TOPK = 3
PAGE_SIZE = 128
PAGES_PER_MOBA_BLOCK = MOBA_BLOCK // PAGE_SIZE
PAGES_PER_STEP = 8
LRU_W = 512
LRU_BLOCKS = 8
CONV_W = 4
RG_C = 8.0
N_BUCKETS = 32
MAX_DIST = 128
RMS_EPS = 1e-6
NEG = -1e30
SCALE = HEAD_DIM ** -0.5
VMEM_LIMIT = 56 * 1024 * 1024

_NT = (((1,), (1,)), ((), ()))


def _params(*sem):
    return pltpu.CompilerParams(dimension_semantics=sem, vmem_limit_bytes=VMEM_LIMIT)


def _rms(x, g):
    return x * lax.rsqrt(jnp.mean(x * x, axis=-1, keepdims=True) + RMS_EPS) * g


def _softplus(z):
    return jnp.maximum(z, 0.0) + jnp.log1p(jnp.exp(-jnp.abs(z)))


def _softplus_abs(z):
    return jnp.maximum(z, 0.0) + jnp.log(1.0 + jnp.exp(-jnp.abs(z)))


def _expm1(x):
    u = jnp.exp(x)
    um1 = u - 1.0
    return jnp.where(um1 == 0.0, x, jnp.where(um1 == -1.0, -1.0, um1 * x / jnp.log(u)))


def _dot_nt(a, b):
    return lax.dot_general(a.astype(BF16), b.astype(BF16), _NT, preferred_element_type=F32)


def _ffn_kernel(x_ref, g_ref, w1_ref, w3_ref, w2_ref, gf_ref, o_ref, h_ref, acc_ref, *, final_norm):
    j = pl.program_id(1)

    @pl.when(j == 0)
    def _():
        h_ref[...] = _rms(x_ref[...], g_ref[...]).astype(BF16)
        acc_ref[...] = jnp.zeros_like(acc_ref)

    h = h_ref[...]
    a = jnp.dot(h, w1_ref[...], preferred_element_type=F32)
    b = jnp.dot(h, w3_ref[...], preferred_element_type=F32)
    u = a * jax.nn.sigmoid(a) * b
    acc_ref[...] += jnp.dot(u.astype(BF16), w2_ref[...], preferred_element_type=F32)

    @pl.when(j == pl.num_programs(1) - 1)
    def _():
        y = x_ref[...] + 0.5 * acc_ref[...]
        if final_norm:
            y = _rms(y, gf_ref[...])
        o_ref[...] = y


def _ffn(x, g, w1, w3, w2, gf, *, final_norm, tm, tf):
    n, d = x.shape
    dff = w1.shape[1]
    return pl.pallas_call(
        functools.partial(_ffn_kernel, final_norm=final_norm),
        grid=(n // tm, dff // tf),
        in_specs=[
            pl.BlockSpec((tm, d), lambda i, j: (i, 0)),
            pl.BlockSpec((1, d), lambda i, j: (0, 0)),
            pl.BlockSpec((d, tf), lambda i, j: (0, j)),
            pl.BlockSpec((d, tf), lambda i, j: (0, j)),
            pl.BlockSpec((tf, d), lambda i, j: (j, 0)),
            pl.BlockSpec((1, d), lambda i, j: (0, 0)),
        ],
        out_specs=pl.BlockSpec((tm, d), lambda i, j: (i, 0)),
        out_shape=jax.ShapeDtypeStruct((n, d), F32),
        scratch_shapes=[pltpu.VMEM((tm, d), BF16), pltpu.VMEM((tm, d), F32)],
        compiler_params=_params("parallel", "arbitrary"),
        name="ffn",
    )(x, g, w1, w3, w2, gf)


N_IN_SPLIT = 8


K_SLOTS = (1, 6)
V_SLOTS = (2, 7)


def _inproj_kernel(x_ref, g_ref, w_ref, *refs, attention_layouts):
    outs, extra, h_ref = refs[:N_IN_SPLIT + 1], refs[N_IN_SPLIT + 1:-1], refs[-1]
    j = pl.program_id(1)

    @pl.when(j == 0)
    def _():
        h_ref[...] = _rms(x_ref[...], g_ref[...]).astype(BF16)

    p = jnp.dot(h_ref[...], w_ref[...], preferred_element_type=F32)
    for k in range(N_IN_SPLIT // 2):
        @pl.when(j == k)
        def _(k=k):
            for slot in (2 * k, 2 * k + 1):
                part = p[:, (slot % 2) * ATT_W:(slot % 2 + 1) * ATT_W]
                outs[slot][...] = part
                if not attention_layouts:
                    continue
                if slot in K_SLOTS:
                    extra[2 * K_SLOTS.index(slot)][...] = part.astype(BF16)
                if slot in V_SLOTS:
                    vt_ref = extra[2 * V_SLOTS.index(slot) + 1]
                    for blk in range(vt_ref.shape[0]):
                        vt_ref[blk] = part[blk * MOBA_BLOCK:(blk + 1) * MOBA_BLOCK, :].T.astype(BF16)

    @pl.when(j >= N_IN_SPLIT // 2)
    def _():
        outs[N_IN_SPLIT][...] = p


def _inproj(x, g, w_in, *, tm, attention_layouts):
    n, d = x.shape
    cols = w_in.shape[1]
    tn = 2 * ATT_W
    n_split_tiles = N_IN_SPLIT // 2
    n_gate_tiles = cols // tn - n_split_tiles
    out_specs = [pl.BlockSpec((tm, ATT_W), lambda i, j: (i, 0)) for _ in range(N_IN_SPLIT)]
    out_specs.append(pl.BlockSpec((tm, tn), lambda i, j: (i, jnp.maximum(j - n_split_tiles, 0))))
    out_shape = [jax.ShapeDtypeStruct((n, ATT_W), F32) for _ in range(N_IN_SPLIT)]
    out_shape.append(jax.ShapeDtypeStruct((n, n_gate_tiles * tn), F32))
    if attention_layouts:
        blocks_per_tile = tm // MOBA_BLOCK
        for _ in range(2):
            out_specs.append(pl.BlockSpec((tm, ATT_W), lambda i, j: (i, 0)))
            out_shape.append(jax.ShapeDtypeStruct((n, ATT_W), BF16))
            out_specs.append(pl.BlockSpec((blocks_per_tile, ATT_W, MOBA_BLOCK), lambda i, j: (i, 0, 0)))
            out_shape.append(jax.ShapeDtypeStruct((n // MOBA_BLOCK, ATT_W, MOBA_BLOCK), BF16))
    return pl.pallas_call(
        functools.partial(_inproj_kernel, attention_layouts=attention_layouts),
        grid=(n // tm, cols // tn),
        in_specs=[
            pl.BlockSpec((tm, d), lambda i, j: (i, 0)),
            pl.BlockSpec((1, d), lambda i, j: (0, 0)),
            pl.BlockSpec((d, tn), lambda i, j: (0, j)),
        ],
        out_specs=out_specs,
        out_shape=out_shape,
        scratch_shapes=[pltpu.VMEM((tm, d), BF16)],
        compiler_params=_params("parallel", "arbitrary"),
        name="inproj",
    )(x, g, w_in)


def _merge_kernel(oa_ref, ob_ref, oc_ref, gl_ref, x_ref, bg_ref, wb_ref, wo_ref, o_ref, *,
                  transposed_attention):
    d = x_ref.shape[1]
    merged = None
    for n, o_n in enumerate((oa_ref, ob_ref, oc_ref)):
        branch = o_n[...]
        if transposed_attention and n != 1:
            branch = branch.T
        proj = jnp.dot(branch.astype(BF16), wb_ref[n], preferred_element_type=F32)
        gate = jax.nn.sigmoid(gl_ref[:, n * d:(n + 1) * d] + bg_ref[n:n + 1, :])
        merged = gate * proj if merged is None else merged + gate * proj
    o_ref[...] = x_ref[...] + jnp.dot(merged.astype(BF16), wo_ref[...], preferred_element_type=F32)


def _merge(oa, ob, oc, gl, x, bg, wb, wo, *, tm, transposed_attention):
    n, d = x.shape
    row = lambda w: pl.BlockSpec((tm, w), lambda i: (i, 0))
    att = pl.BlockSpec((None, ATT_W, tm), lambda i: (i, 0, 0)) if transposed_attention else row(ATT_W)
    return pl.pallas_call(
        functools.partial(_merge_kernel, transposed_attention=transposed_attention),
        grid=(n // tm,),
        in_specs=[att, row(ATT_W), att, row(gl.shape[1]), row(d),
                  pl.BlockSpec(bg.shape, lambda i: (0, 0)),
                  pl.BlockSpec(wb.shape, lambda i: (0, 0, 0)),
                  pl.BlockSpec(wo.shape, lambda i: (0, 0))],
        out_specs=row(d),
        out_shape=jax.ShapeDtypeStruct((n, d), F32),
        compiler_params=_params("parallel"),
        name="merge",
    )(oa, ob, oc, gl, x, bg, wb, wo)


def _rel_bucket(dist):
    max_exact = N_BUCKETS // 2
    d = jnp.maximum(dist, 0)
    df = jnp.maximum(d, 1).astype(F32)
    large = max_exact + (jnp.log(df / max_exact) / math.log(MAX_DIST / max_exact)
                         * (N_BUCKETS - max_exact)).astype(jnp.int32)
    large = jnp.minimum(large, N_BUCKETS - 1)
    return jnp.where(d < max_exact, d, large)


def _bias_tables(rel_bias):
    period = 4 * MOBA_BLOCK
    k = jnp.arange(period, dtype=jnp.int32)
    k = jnp.where(k < period // 2, k, k - period)
    u = rel_bias[_rel_bucket(jnp.maximum(MOBA_BLOCK + k, 0))].T
    n_keys = 2 * MOBA_BLOCK
    toeplitz = jnp.tile(u, (1, n_keys))[:, :n_keys * (period - 1)]
    toeplitz = toeplitz.reshape(N_HEADS, n_keys, period - 1)[:, :, :MOBA_BLOCK]
    heads_per_step = STEP_W // HEAD_DIM
    n_steps = N_HEADS // heads_per_step
    toeplitz = toeplitz.reshape(n_steps, heads_per_step, n_keys, MOBA_BLOCK)
    toeplitz = jnp.transpose(toeplitz, (0, 2, 1, 3)).reshape(n_steps, n_keys, STEP_Q)
    far = rel_bias[_rel_bucket(jnp.int32(2 * MOBA_BLOCK))]
    far_pair = jnp.repeat(far.reshape(n_steps, 1, heads_per_step), MOBA_BLOCK, axis=2)
    dec = rel_bias[_rel_bucket(MOBA_BLOCK - jnp.arange(MOBA_BLOCK, dtype=jnp.int32))].T
    own0 = rel_bias[_rel_bucket(jnp.int32(0))][:, None]
    return toeplitz, far_pair, dec, far[:, None], own0


PAIR_Q = HEADS_PER_TILE * MOBA_BLOCK
TILES_PER_STEP = 4
STEP_W = TILES_PER_STEP * LANES
STEP_Q = TILES_PER_STEP * PAIR_Q


def _stacked_queries(q):
    lane = lax.broadcasted_iota(jnp.int32, (1, LANES), 1)
    stacks = []
    for g in range(TILES_PER_STEP):
        qs = q[:, g * LANES:(g + 1) * LANES] * SCALE
        stacks.append(jnp.concatenate([jnp.where((lane // HEAD_DIM) == hh, qs, 0.0).astype(BF16)
                                       for hh in range(HEADS_PER_TILE)], axis=0))
    return stacks


def _scores(k_blk, qcat):
    return jnp.concatenate([_dot_nt(k_blk[:, g * LANES:(g + 1) * LANES], qcat[g])
                            for g in range(TILES_PER_STEP)], axis=1)


def _values(vt_blk, w):
    return jnp.concatenate([jnp.dot(vt_blk[g * LANES:(g + 1) * LANES, :], w[:, g * PAIR_Q:(g + 1) * PAIR_Q],
                                    preferred_element_type=F32)
                            for g in range(TILES_PER_STEP)], axis=0)


def _per_head_rows(stat):
    return jnp.concatenate([jnp.broadcast_to(stat[:, g * PAIR_Q:(g + 1) * PAIR_Q], (LANES, PAIR_Q))
                            for g in range(TILES_PER_STEP)], axis=0)


def _pair_corners(acc):
    rows = []
    for r in range(STEP_W // HEAD_DIM):
        hh = r % HEADS_PER_TILE
        rows.append(acc[r * HEAD_DIM:(r + 1) * HEAD_DIM, hh * MOBA_BLOCK:(hh + 1) * MOBA_BLOCK])
    return jnp.concatenate(rows, axis=0)


def _moba_prompt_kernel(q_ref, kf_ref, kb_ref, vt_ref, tb_ref, far_ref, o_ref, kmean_ref, sel_ref):
    i = pl.program_id(2)
    nb = kb_ref.shape[0] // MOBA_BLOCK

    @pl.when(i == 0)
    def _():
        for n in range(nb):
            kmean_ref[n:n + 1, :] = jnp.mean(kf_ref[n * MOBA_BLOCK:(n + 1) * MOBA_BLOCK, :],
                                             axis=0, keepdims=True)

    q = q_ref[...]
    qcat = _stacked_queries(q)
    lane = lax.broadcasted_iota(jnp.int32, (1, LANES), 1)
    blk_row = lax.broadcasted_iota(jnp.int32, (nb, 1), 0)

    gate = jnp.concatenate(
        [lax.dot_general(jnp.where((lane // HEAD_DIM) == hh, kmean_ref[:, g * LANES:(g + 1) * LANES], 0.0),
                         q[:, g * LANES:(g + 1) * LANES], _NT,
                         precision=lax.Precision.HIGHEST, preferred_element_type=F32)
         for g in range(TILES_PER_STEP) for hh in range(HEADS_PER_TILE)], axis=1)
    gate = jnp.where(blk_row < i, gate, NEG)
    rank = jnp.zeros_like(gate)
    for m in range(nb):
        gm = gate[m:m + 1, :]
        ahead = (gm > gate) | ((gm == gate) & (blk_row > m))
        rank = rank + ahead.astype(F32)
    sel_ref[...] = ((rank < MOBA_TOPK) & (blk_row < i)).astype(F32)

    def block(n, bias, keep, m_run, l_run, acc):
        start = pl.multiple_of(n * MOBA_BLOCK, MOBA_BLOCK)
        s = _scores(kb_ref[pl.ds(start, MOBA_BLOCK), :], qcat) + bias
        s = jnp.where(keep, s, NEG)
        m_new = jnp.max(s, axis=0, keepdims=True)
        if m_run is not None:
            m_new = jnp.maximum(m_run, m_new)
        p = jnp.exp(s - m_new)
        l_new = jnp.sum(p, axis=0, keepdims=True)
        pv = _values(vt_ref[n], p.astype(BF16))
        if m_run is None:
            return m_new, l_new, pv
        alpha = jnp.exp(m_run - m_new)
        return m_new, alpha * l_run + l_new, _per_head_rows(alpha) * acc + pv

    key_i = lax.broadcasted_iota(jnp.int32, (MOBA_BLOCK, STEP_Q), 0)
    qry_i = lax.broadcasted_iota(jnp.int32, (MOBA_BLOCK, STEP_Q), 1) % MOBA_BLOCK
    state0 = block(i, tb_ref[MOBA_BLOCK:, :], key_i <= qry_i, None, None, None)

    def body(n, state):
        bias = jnp.where(n == i - 1, tb_ref[:MOBA_BLOCK, :], far_ref[...])
        return block(n, bias, sel_ref[pl.ds(n, 1), :] > 0.5, *state)

    _, l_fin, acc_fin = lax.fori_loop(0, i, body, state0)
    o_ref[...] = _pair_corners(acc_fin / _per_head_rows(l_fin))


def _moba_prompt(q, k_f32, k_bf, vt, toeplitz, far, *, batch, seq):
    nb = seq // MOBA_BLOCK
    return pl.pallas_call(
        _moba_prompt_kernel,
        grid=(batch, ATT_W // STEP_W, nb),
        in_specs=[
            pl.BlockSpec((MOBA_BLOCK, STEP_W), lambda b, p, i: (b * nb + i, p)),
            pl.BlockSpec((seq, STEP_W), lambda b, p, i: (b, p)),
            pl.BlockSpec((seq, STEP_W), lambda b, p, i: (b, p)),
            pl.BlockSpec((nb, STEP_W, MOBA_BLOCK), lambda b, p, i: (b, p, 0)),
            pl.BlockSpec((None, 2 * MOBA_BLOCK, STEP_Q), lambda b, p, i: (p, 0, 0)),
            pl.BlockSpec((None, 1, STEP_Q), lambda b, p, i: (p, 0, 0)),
        ],
        out_specs=pl.BlockSpec((None, STEP_W, MOBA_BLOCK), lambda b, p, i: (b * nb + i, p, 0)),
        out_shape=jax.ShapeDtypeStruct((batch * nb, ATT_W, MOBA_BLOCK), F32),
        scratch_shapes=[pltpu.VMEM((nb, STEP_W), F32), pltpu.VMEM((nb, STEP_Q), F32)],
        compiler_params=_params("parallel", "parallel", "arbitrary"),
        name="moba_prompt",
    )(q, k_f32, k_bf, vt, toeplitz, far)


SB_BLOCK = 256


def _sb_prompt_kernel(q_ref, kb_ref, vt_ref, o_ref):
    i = pl.program_id(2)
    qcat = _stacked_queries(q_ref[...])
    row_i = lax.broadcasted_iota(jnp.int32, (SB_BLOCK, SB_BLOCK), 0)
    col_i = lax.broadcasted_iota(jnp.int32, (SB_BLOCK, SB_BLOCK), 1)
    later = (row_i < col_i).astype(BF16)

    def block(n, mask, carry):
        start = pl.multiple_of(n * SB_BLOCK, SB_BLOCK)
        z = _scores(kb_ref[pl.ds(start, SB_BLOCK), :], qcat)
        sp = _softplus_abs(z)
        log_keep = -sp if mask is None else jnp.where(mask, -sp, 0.0)
        hi = log_keep.astype(BF16)
        lo = (log_keep - hi.astype(F32)).astype(BF16)
        both = jnp.dot(later, jnp.concatenate([hi, lo], axis=1), preferred_element_type=F32)
        between = both[:, :STEP_Q] + both[:, STEP_Q:]
        if carry is not None:
            between = between + carry
        w = jnp.exp(z - sp + between)
        if mask is not None:
            w = jnp.where(mask, w, 0.0)
        pv = _values(vt_ref[n], w.astype(BF16))
        return jnp.sum(log_keep, axis=0, keepdims=True), pv

    key_i = lax.broadcasted_iota(jnp.int32, (SB_BLOCK, STEP_Q), 0)
    qry_i = lax.broadcasted_iota(jnp.int32, (SB_BLOCK, STEP_Q), 1) % SB_BLOCK
    state0 = block(i, key_i < qry_i, None)

    def body(t, state):
        carry, acc = state
        kept, pv = block(i - 1 - t, None, carry)
        return carry + kept, acc + pv

    _, acc_fin = lax.fori_loop(0, i, body, state0)
    o_ref[...] = _pair_corners(acc_fin)


def _sb_prompt(q, k_bf, vt, *, batch, seq):
    nb = seq // SB_BLOCK
    return pl.pallas_call(
        _sb_prompt_kernel,
        grid=(batch, ATT_W // STEP_W, nb),
        in_specs=[
            pl.BlockSpec((SB_BLOCK, STEP_W), lambda b, p, i: (b * nb + i, p)),
            pl.BlockSpec((seq, STEP_W), lambda b, p, i: (b, p)),
            pl.BlockSpec((nb, STEP_W, SB_BLOCK), lambda b, p, i: (b, p, 0)),
        ],
        out_specs=pl.BlockSpec((None, STEP_W, SB_BLOCK), lambda b, p, i: (b * nb + i, p, 0)),
        out_shape=jax.ShapeDtypeStruct((batch * nb, ATT_W, SB_BLOCK), F32),
        compiler_params=_params("parallel", "parallel", "arbitrary"),
        name="sb_prompt",
    )(q, k_bf, vt)


def _lru_gates(xc, wa_ref, ba_ref, wx_ref, bx_ref, lam_ref):
    xc_bf = xc.astype(BF16)
    r = jax.nn.sigmoid(jnp.dot(xc_bf, wa_ref[...], preferred_element_type=F32) + ba_ref[...])
    gi = jax.nn.sigmoid(jnp.dot(xc_bf, wx_ref[...], preferred_element_type=F32) + bx_ref[...])
    log_a = -RG_C * r * _softplus(-lam_ref[...])
    a = jnp.exp(log_a)
    b = jnp.sqrt(-_expm1(2.0 * log_a)) * (gi * xc)
    return a, b


def _lru_prompt_kernel(x_ref, g_ref, buf_ref, h0_ref, cw_ref, cb_ref, wa_ref, ba_ref, wx_ref, bx_ref,
                       lam_ref, y_ref, hlast_ref, conv_ref, tail_ref, hc_ref):
    t = pl.program_id(1)
    tc = x_ref.shape[0]
    tail_rows = tail_ref.shape[0]

    @pl.when(t == 0)
    def _():
        tail_ref[...] = jnp.zeros_like(tail_ref)
        tail_ref[tail_rows - (CONV_W - 1):, :] = buf_ref[...]
        hc_ref[...] = h0_ref[...]

    x = x_ref[...]
    xe = jnp.concatenate([tail_ref[...], x], axis=0)
    xc = cb_ref[...] + cw_ref[CONV_W - 1:CONV_W, :] * x
    for k in range(1, CONV_W):
        xc = xc + cw_ref[CONV_W - 1 - k:CONV_W - k, :] * pltpu.roll(xe, k, 0)[tail_rows:, :]
    a, b = _lru_gates(xc, wa_ref, ba_ref, wx_ref, bx_ref, lam_ref)

    row = lax.broadcasted_iota(jnp.int32, (tc, 1), 0)
    k = 1
    while k < tc:
        a_sh = jnp.where(row >= k, pltpu.roll(a, k, 0), 1.0)
        b_sh = jnp.where(row >= k, pltpu.roll(b, k, 0), 0.0)
        b = a * b_sh + b
        a = a * a_sh
        k *= 2
    h = a * hc_ref[...] + b
    y_ref[...] = h * jax.nn.gelu(g_ref[...], approximate=True)
    hc_ref[...] = h[tc - 1:tc, :]
    tail_ref[...] = x[tc - tail_rows:, :]

    @pl.when(t == pl.num_programs(1) - 1)
    def _():
        hlast_ref[...] = h[tc - 1:tc, :]
        conv_ref[...] = x[tc - (CONV_W - 1):, :]


def _lru_prompt(x, g, buf, h0, lw, *, batch, seq, tc):
    nt = seq // tc
    w = x.shape[1]
    vec = pl.BlockSpec((1, w), lambda b, t: (0, 0))
    mat = pl.BlockSpec((w, w), lambda b, t: (0, 0))
    return pl.pallas_call(
        _lru_prompt_kernel,
        grid=(batch, nt),
        in_specs=[
            pl.BlockSpec((tc, w), lambda b, t: (b * nt + t, 0)),
            pl.BlockSpec((tc, w), lambda b, t: (b * nt + t, 0)),
            pl.BlockSpec((None, CONV_W - 1, w), lambda b, t: (b, 0, 0)),
            pl.BlockSpec((None, 1, w), lambda b, t: (b, 0, 0)),
            pl.BlockSpec((CONV_W, w), lambda b, t: (0, 0)),
            vec, mat, vec, mat, vec, vec,
        ],
        out_specs=[
            pl.BlockSpec((tc, w), lambda b, t: (b * nt + t, 0)),
            pl.BlockSpec((None, 1, w), lambda b, t: (b, 0, 0)),
            pl.BlockSpec((None, CONV_W - 1, w), lambda b, t: (b, 0, 0)),
        ],
        out_shape=[jax.ShapeDtypeStruct(x.shape, F32),
                   jax.ShapeDtypeStruct((batch, 1, w), F32),
                   jax.ShapeDtypeStruct((batch, CONV_W - 1, w), F32)],
        scratch_shapes=[pltpu.VMEM((8, w), F32), pltpu.VMEM((1, w), F32)],
        compiler_params=_params("parallel", "arbitrary"),
        name="lru_prompt",
    )(x, g, buf, h0, lw["conv_w"], lw["conv_b"], lw["wa"], lw["ba"], lw["wx"], lw["bx"], lw["lam"])


def _lru_step_kernel(x_ref, g_ref, buf_ref, h0_ref, cw_ref, cb_ref, wa_ref, ba_ref, wx_ref, bx_ref,
                     lam_ref, y_ref, h_ref, conv_ref):
    x = x_ref[...]
    xc = cb_ref[...] + cw_ref[CONV_W - 1:CONV_W, :] * x
    for k in range(CONV_W - 1):
        xc = xc + cw_ref[k:k + 1, :] * buf_ref[k]
    a, b = _lru_gates(xc, wa_ref, ba_ref, wx_ref, bx_ref, lam_ref)
    h = a * h0_ref[...] + b
    y_ref[...] = h * jax.nn.gelu(g_ref[...], approximate=True)
    h_ref[...] = h
    for k in range(CONV_W - 2):
        conv_ref[k] = buf_ref[k + 1]
    conv_ref[CONV_W - 2] = x


def _lru_step(x, g, buf, h0, lw):
    n, w = x.shape
    return pl.pallas_call(
        _lru_step_kernel,
        out_shape=[jax.ShapeDtypeStruct((n, w), F32), jax.ShapeDtypeStruct((n, w), F32),
                   jax.ShapeDtypeStruct((CONV_W - 1, n, w), F32)],
        compiler_params=pltpu.CompilerParams(vmem_limit_bytes=VMEM_LIMIT),
        name="lru_step",
    )(x, g, buf, h0, lw["conv_w"], lw["conv_b"], lw["wa"], lw["ba"], lw["wx"], lw["bx"], lw["lam"])


def _page_scores(qb, kt_ref):
    return jnp.concatenate([jnp.sum(qb[h] * kt_ref[h], axis=0, keepdims=True)
                            for h in range(N_HEADS)], axis=0)


def _weighted_values(w, vt_ref):
    return jnp.stack([jnp.broadcast_to(w[h:h + 1, :], (HEAD_DIM, PAGE_SIZE)) * vt_ref[h]
                      for h in range(N_HEADS)], axis=0)


def _sum_positions(acc):
    ones = jnp.ones((N_HEADS, PAGE_SIZE), F32)
    rows = [lax.dot_general(ones, acc[h], _NT, precision=lax.Precision.HIGHEST,
                            preferred_element_type=F32)[:1] for h in range(N_HEADS)]
    return jnp.concatenate(rows, axis=0)


def _moba_decode_kernel(pt_ref, q_ref, q2_ref, kn_ref, vn_ref, dec_ref, far_ref, own_ref, *refs):
    del pt_ref
    n_pg = PAGES_PER_STEP
    k_pages, v_pages = refs[:n_pg], refs[n_pg:2 * n_pg]
    o_ref, gate_ref, m_ref, l_ref, acc_ref = refs[2 * n_pg:]
    c = pl.program_id(1)
    n_steps = pl.num_programs(1)
    blocks_per_step = n_pg // PAGES_PER_MOBA_BLOCK
    qb = jnp.broadcast_to(q_ref[...] * SCALE, (N_HEADS, HEAD_DIM, PAGE_SIZE))
    lane = lax.broadcasted_iota(jnp.int32, (1, LANES), 1)

    @pl.when(c == 0)
    def _():
        gate_ref[...] = jnp.full_like(gate_ref, NEG)
        m_ref[...] = jnp.full_like(m_ref, NEG)
        l_ref[...] = jnp.zeros_like(l_ref)

    for j in range(blocks_per_step):
        n = c * blocks_per_step + j
        pages = range(PAGES_PER_MOBA_BLOCK * j, PAGES_PER_MOBA_BLOCK * (j + 1))
        z = [_page_scores(qb, k_pages[pg]) for pg in pages]
        g = sum(jnp.sum(zt, axis=-1, keepdims=True) for zt in z) * (1.0 / (SCALE * MOBA_BLOCK))
        newest = n == n_steps * blocks_per_step - 1
        s = [zt + jnp.where(newest, dec_ref[:, t * PAGE_SIZE:(t + 1) * PAGE_SIZE], far_ref[...])
             for t, zt in enumerate(z)]
        m = functools.reduce(jnp.maximum, [jnp.max(st, axis=-1, keepdims=True) for st in s])
        p = [jnp.exp(st - m) for st in s]
        l = sum(jnp.sum(pt, axis=-1, keepdims=True) for pt in p)
        acc_ref[n] = sum(_weighted_values(pt, v_pages[pg]) for pt, pg in zip(p, pages))
        gate_ref[...] = jnp.where(lane == n, g, gate_ref[...])
        m_ref[...] = jnp.where(lane == n, m, m_ref[...])
        l_ref[...] = jnp.where(lane == n, l, l_ref[...])

    @pl.when(c == n_steps - 1)
    def _():
        nb = n_steps * blocks_per_step
        gate = gate_ref[...]
        rank = jnp.zeros_like(gate)
        for mm in range(nb):
            gm = gate[:, mm:mm + 1]
            ahead = (gm > gate) | ((gm == gate) & (lane > mm))
            rank = rank + ahead.astype(F32)
        sel = (rank < MOBA_TOPK) & (lane < nb)
        s_own = jnp.sum(q2_ref[...] * SCALE * kn_ref[...], axis=-1, keepdims=True) + own_ref[...]
        m_sel = jnp.where(sel, m_ref[...], NEG)
        m_tot = jnp.maximum(jnp.max(m_sel, axis=-1, keepdims=True), s_own)
        wgt = jnp.where(sel, jnp.exp(m_sel - m_tot), 0.0)
        w_own = jnp.exp(s_own - m_tot)
        l_tot = jnp.sum(wgt * l_ref[...], axis=-1, keepdims=True) + w_own
        merged = []
        for h in range(N_HEADS):
            tot = jnp.zeros((HEAD_DIM, PAGE_SIZE), F32)
            for mm in range(nb):
                tot = tot + jnp.broadcast_to(wgt[h:h + 1, mm:mm + 1], (HEAD_DIM, PAGE_SIZE)) * acc_ref[mm, h]
            merged.append(tot)
        o = w_own * vn_ref[...] + _sum_positions(merged)
        o_ref[...] = o / l_tot


def _page_spec(layer, slot, order):
    def index_map(b, c, pt):
        n_pages = pt.shape[1]
        pos = c * PAGES_PER_STEP + slot
        return (layer, pt[b, pos if order > 0 else n_pages - 1 - pos], 0, 0, 0)
    return pl.BlockSpec((None, None, N_HEADS, HEAD_DIM, PAGE_SIZE), index_map)


def _moba_decode(page_table, q, k_new, v_new, cache_k, cache_v, dec, far, own0, *, layer):
    batch, n_pages = page_table.shape
    n_steps = n_pages // PAGES_PER_STEP
    nb = n_pages // PAGES_PER_MOBA_BLOCK
    heads = pl.BlockSpec((None, N_HEADS, HEAD_DIM), lambda b, c, pt: (b, 0, 0))
    column = pl.BlockSpec((None, N_HEADS, HEAD_DIM, 1), lambda b, c, pt: (b, 0, 0, 0))
    const = lambda a: pl.BlockSpec(a.shape, lambda b, c, pt: (0,) * a.ndim)
    grid_spec = pltpu.PrefetchScalarGridSpec(
        num_scalar_prefetch=1,
        grid=(batch, n_steps),
        in_specs=[column, heads, heads, heads, const(dec), const(far), const(own0)]
        + [_page_spec(layer, s, +1) for s in range(PAGES_PER_STEP)]
        + [_page_spec(layer, s, +1) for s in range(PAGES_PER_STEP)],
        out_specs=heads,
        scratch_shapes=[pltpu.VMEM((N_HEADS, LANES), F32), pltpu.VMEM((N_HEADS, LANES), F32),
                        pltpu.VMEM((N_HEADS, LANES), F32),
                        pltpu.VMEM((nb, N_HEADS, HEAD_DIM, PAGE_SIZE), F32)],
    )
    return pl.pallas_call(
        _moba_decode_kernel,
        grid_spec=grid_spec,
        out_shape=jax.ShapeDtypeStruct((batch, N_HEADS, HEAD_DIM), F32),
        compiler_params=_params("parallel", "arbitrary"),
        name="moba_decode",
    )(page_table, q[..., None], q, k_new, v_new, dec, far, own0,
      *([cache_k] * PAGES_PER_STEP), *([cache_v] * PAGES_PER_STEP))


def _sb_decode_kernel(pt_ref, q_ref, *refs):
    del pt_ref
    n_pg = PAGES_PER_STEP
    k_pages, v_pages = refs[:n_pg], refs[n_pg:2 * n_pg]
    o_ref, carry_ref, acc_ref = refs[2 * n_pg:]
    c = pl.program_id(1)
    qb = jnp.broadcast_to(q_ref[...] * SCALE, (N_HEADS, HEAD_DIM, PAGE_SIZE))
    lane = lax.broadcasted_iota(jnp.int32, (1, PAGE_SIZE), 1)

    @pl.when(c == 0)
    def _():
        carry_ref[...] = jnp.zeros_like(carry_ref)
        acc_ref[...] = jnp.zeros_like(acc_ref)

    carry = carry_ref[...]
    for j in range(n_pg):
        z = _page_scores(qb, k_pages[j])
        sp = _softplus_abs(z)
        log_keep = -sp
        incl = log_keep
        shift = 1
        while shift < PAGE_SIZE:
            moved = pltpu.roll(incl, PAGE_SIZE - shift, 1)
            incl = incl + jnp.where(lane < PAGE_SIZE - shift, moved, 0.0)
            shift *= 2
        w = jnp.exp(z - sp + incl - log_keep + carry)
        acc_ref[...] += _weighted_values(w, v_pages[j])
        carry = carry + jnp.sum(log_keep, axis=-1, keepdims=True)
    carry_ref[...] = carry

    @pl.when(c == pl.num_programs(1) - 1)
    def _():
        o_ref[...] = _sum_positions(acc_ref[...])


def _sb_decode(page_table, q, cache_k, cache_v, *, layer):
    batch, n_pages = page_table.shape
    n_steps = n_pages // PAGES_PER_STEP
    grid_spec = pltpu.PrefetchScalarGridSpec(
        num_scalar_prefetch=1,
        grid=(batch, n_steps),
        in_specs=[pl.BlockSpec((None, N_HEADS, HEAD_DIM, 1), lambda b, c, pt: (b, 0, 0, 0))]
        + [_page_spec(layer, s, -1) for s in range(PAGES_PER_STEP)]
        + [_page_spec(layer, s, -1) for s in range(PAGES_PER_STEP)],
        out_specs=pl.BlockSpec((None, N_HEADS, HEAD_DIM), lambda b, c, pt: (b, 0, 0)),
        scratch_shapes=[pltpu.VMEM((N_HEADS, 1), F32),
                        pltpu.VMEM((N_HEADS, HEAD_DIM, PAGE_SIZE), F32)],
    )
    return pl.pallas_call(
        _sb_decode_kernel,
        grid_spec=grid_spec,
        out_shape=jax.ShapeDtypeStruct((batch, N_HEADS, HEAD_DIM), F32),
        compiler_params=_params("parallel", "arbitrary"),
        name="sb_decode",
    )(page_table, q[..., None], *([cache_k] * PAGES_PER_STEP), *([cache_v] * PAGES_PER_STEP))


def _block_diag(w):
    nblk, bk, bj = w.shape
    eye = jnp.eye(nblk, dtype=w.dtype)
    return jnp.einsum("nkj,nm->nkmj", w, eye).reshape(nblk * bk, nblk * bj)


def _layer_weights(l, norm_ffn1, ffn1_w1, ffn1_w3, ffn1_w2, norm_mix, w_in, b_gate, conv_w, conv_b,
                   lru_wa, lru_ba, lru_wx, lru_bx, lru_lambda, w_branch, w_out, norm_ffn2,
                   ffn2_w1, ffn2_w3, ffn2_w2):
    vec = lambda a: a[l].reshape(1, -1)
    bf = lambda a: a[l].astype(BF16)
    return {
        "ln1": vec(norm_ffn1), "f1_w1": bf(ffn1_w1), "f1_w3": bf(ffn1_w3), "f1_w2": bf(ffn1_w2),
        "ln_mix": vec(norm_mix), "w_in": bf(w_in), "b_gate": b_gate[l],
        "lru": {"conv_w": conv_w[l], "conv_b": vec(conv_b),
                "wa": _block_diag(lru_wa[l]).astype(BF16), "ba": vec(lru_ba),
                "wx": _block_diag(lru_wx[l]).astype(BF16), "bx": vec(lru_bx), "lam": vec(lru_lambda)},
        "w_branch": bf(w_branch), "w_out": bf(w_out),
        "ln2": vec(norm_ffn2), "f2_w1": bf(ffn2_w1), "f2_w3": bf(ffn2_w3), "f2_w2": bf(ffn2_w2),
    }


FFN_TILE_F = 256


def kernel(x_prompt, x_sample, cache_moba_k, cache_moba_v, cache_sb_k, cache_sb_v, state_lru_h,
           state_lru_conv, page_table, rel_bias, norm_ffn1, ffn1_w1, ffn1_w3, ffn1_w2, norm_mix, w_in,
           b_gate, conv_w, conv_b, lru_wa, lru_ba, lru_wx, lru_bx, lru_lambda, w_branch, w_out,
           norm_ffn2, ffn2_w1, ffn2_w3, ffn2_w2, norm_final):
    bp, seq, d = x_prompt.shape
    db = x_sample.shape[0]
    depth = w_in.shape[0]
    n_phys = cache_moba_k.shape[1]
    toeplitz, far, dec, far_dec, own0 = _bias_tables(rel_bias)
    gf = norm_final.reshape(1, d)
    paged = lambda c: jnp.transpose(c, (0, 1, 3, 4, 2))
    cmk, cmv, csk, csv = paged(cache_moba_k), paged(cache_moba_v), paged(cache_sb_k), paged(cache_sb_v)

    yp = x_prompt.reshape(bp * seq, d)
    ys = x_sample.reshape(db, d)
    tm_p = 1024 if (bp * seq) % 1024 == 0 else MOBA_BLOCK
    zeros_buf = jnp.zeros((bp, CONV_W - 1, LRU_W), F32)
    zeros_h = jnp.zeros((bp, 1, LRU_W), F32)
    outs = {k: [] for k in ("p_mk", "p_mv", "p_sk", "p_sv", "p_h", "p_cv",
                            "s_mk", "s_mv", "s_sk", "s_sv", "s_h", "s_cv")}
    for l in range(depth):
        lw = _layer_weights(l, norm_ffn1, ffn1_w1, ffn1_w3, ffn1_w2, norm_mix, w_in, b_gate, conv_w,
                            conv_b, lru_wa, lru_ba, lru_wx, lru_bx, lru_lambda, w_branch, w_out,
                            norm_ffn2, ffn2_w1, ffn2_w3, ffn2_w2)
        last = l == depth - 1

        yp = _ffn(yp, lw["ln1"], lw["f1_w1"], lw["f1_w3"], lw["f1_w2"], gf,
                  final_norm=False, tm=tm_p, tf=FFN_TILE_F)
        qa, ka, va, xb, gb, qc, kc, vc, gl, ka_bf, vat, kc_bf, vct = _inproj(
            yp, lw["ln_mix"], lw["w_in"], tm=max(tm_p // 2, MOBA_BLOCK), attention_layouts=True)
        o_a = _moba_prompt(qa, ka, ka_bf, vat, toeplitz, far, batch=bp, seq=seq)
        o_b, h_new, conv_new = _lru_prompt(xb, gb, zeros_buf, zeros_h, lw["lru"],
                                           batch=bp, seq=seq, tc=min(seq, 512))
        o_c = _sb_prompt(qc, kc_bf, vct, batch=bp, seq=seq)
        yp = _merge(o_a, o_b, o_c, gl, yp, lw["b_gate"], lw["w_branch"], lw["w_out"], tm=MOBA_BLOCK,
                    transposed_attention=True)
        yp = _ffn(yp, lw["ln2"], lw["f2_w1"], lw["f2_w3"], lw["f2_w2"], gf,
                  final_norm=last, tm=tm_p, tf=FFN_TILE_F)
        kv_shape = (bp, seq, N_HEADS, HEAD_DIM)
        outs["p_mk"].append(ka.reshape(kv_shape)); outs["p_mv"].append(va.reshape(kv_shape))
        outs["p_sk"].append(kc.reshape(kv_shape)); outs["p_sv"].append(vc.reshape(kv_shape))
        outs["p_h"].append(h_new.reshape(bp, LRU_W)); outs["p_cv"].append(conv_new)

        ys = _ffn(ys, lw["ln1"], lw["f1_w1"], lw["f1_w3"], lw["f1_w2"], gf,
                  final_norm=False, tm=db, tf=FFN_TILE_F)
        qa, ka, va, xb, gb, qc, kc, vc, gl = _inproj(ys, lw["ln_mix"], lw["w_in"], tm=db,
                                                     attention_layouts=False)
        by_head = lambda a: a.reshape(db, N_HEADS, HEAD_DIM)
        o_a = _moba_decode(page_table, by_head(qa), by_head(ka), by_head(va), cmk, cmv,
                           dec, far_dec, own0, layer=l).reshape(db, ATT_W)
        o_b, h_new, conv_new = _lru_step(xb, gb, jnp.transpose(state_lru_conv[l], (1, 0, 2)),
                                         state_lru_h[l], lw["lru"])
        o_c = _sb_decode(page_table, by_head(qc), csk, csv, layer=l).reshape(db, ATT_W)
        ys = _merge(o_a, o_b, o_c, gl, ys, lw["b_gate"], lw["w_branch"], lw["w_out"], tm=db,
                    transposed_attention=False)
        ys = _ffn(ys, lw["ln2"], lw["f2_w1"], lw["f2_w3"], lw["f2_w2"], gf,
                  final_norm=last, tm=db, tf=FFN_TILE_F)
        kv_shape = (db, 1, N_HEADS, HEAD_DIM)
        outs["s_mk"].append(ka.reshape(kv_shape)); outs["s_mv"].append(va.reshape(kv_shape))
        outs["s_sk"].append(kc.reshape(kv_shape)); outs["s_sv"].append(vc.reshape(kv_shape))
        outs["s_h"].append(h_new); outs["s_cv"].append(jnp.transpose(conv_new, (1, 0, 2)))

    st = {k: jnp.stack(v) for k, v in outs.items()}
    return (yp.reshape(bp, seq, d), ys.reshape(db, 1, d),
            st["p_mk"], st["p_mv"], st["p_sk"], st["p_sv"], st["p_h"], st["p_cv"],
            st["s_mk"], st["s_mv"], st["s_sk"], st["s_sv"], st["s_h"], st["s_cv"])
```

```python
import functools
import math

import jax
import jax.numpy as jnp
from jax import lax
from jax.experimental import pallas as pl
from jax.experimental.pallas import tpu as pltpu

F32 = jnp.float32
BF16 = jnp.bfloat16

HEAD_DIM = 64
N_HEADS = 8
ATT_W = N_HEADS * HEAD_DIM
LANES = 128
HEADS_PER_TILE = LANES // HEAD_DIM
N_PAIRS = ATT_W // LANES
MOBA_BLOCK = 256
MOBA_TOPK = 3
PAGE_SIZE = 128
PAGES_PER_MOBA_BLOCK = MOBA_BLOCK // PAGE_SIZE
PAGES_PER_STEP = 16
LRU_W = 512
LRU_BLOCKS = 8
CONV_W = 4
RG_C = 8.0
N_BUCKETS = 32
MAX_DIST = 128
RMS_EPS = 1e-6
NEG = -1e30
SCALE = HEAD_DIM ** -0.5
VMEM_LIMIT = 56 * 1024 * 1024

_NT = (((1,), (1,)), ((), ()))


def _params(*sem):
    return pltpu.CompilerParams(dimension_semantics=sem, vmem_limit_bytes=VMEM_LIMIT)


def _rms(x, g):
    return x * lax.rsqrt(jnp.mean(x * x, axis=-1, keepdims=True) + RMS_EPS) * g


def _softplus(z):
    return jnp.maximum(z, 0.0) + jnp.log1p(jnp.exp(-jnp.abs(z)))


def _softplus_abs(z):
    return jnp.maximum(z, 0.0) + jnp.log(1.0 + jnp.exp(-jnp.abs(z)))


def _expm1(x):
    u = jnp.exp(x)
    um1 = u - 1.0
    return jnp.where(um1 == 0.0, x, jnp.where(um1 == -1.0, -1.0, um1 * x / jnp.log(u)))


def _dot_nt(a, b):
    return lax.dot_general(a.astype(BF16), b.astype(BF16), _NT, preferred_element_type=F32)


def _ffn_kernel(x_ref, g_ref, w1_ref, w3_ref, w2_ref, gf_ref, o_ref, h_ref, acc_ref, *, final_norm):
    j = pl.program_id(1)

    @pl.when(j == 0)
    def _():
        h_ref[...] = _rms(x_ref[...], g_ref[...]).astype(BF16)
        acc_ref[...] = jnp.zeros_like(acc_ref)

    h = h_ref[...]
    a = jnp.dot(h, w1_ref[...], preferred_element_type=F32)
    b = jnp.dot(h, w3_ref[...], preferred_element_type=F32)
    u = a * jax.nn.sigmoid(a) * b
    acc_ref[...] += jnp.dot(u.astype(BF16), w2_ref[...], preferred_element_type=F32)

    @pl.when(j == pl.num_programs(1) - 1)
    def _():
        y = x_ref[...] + 0.5 * acc_ref[...]
        if final_norm:
            y = _rms(y, gf_ref[...])
        o_ref[...] = y


def _ffn(x, g, w1, w3, w2, gf, *, final_norm, tm, tf):
    n, d = x.shape
    dff = w1.shape[1]
    return pl.pallas_call(
        functools.partial(_ffn_kernel, final_norm=final_norm),
        grid=(n // tm, dff // tf),
        in_specs=[
            pl.BlockSpec((tm, d), lambda i, j: (i, 0)),
            pl.BlockSpec((1, d), lambda i, j: (0, 0)),
            pl.BlockSpec((d, tf), lambda i, j: (0, j)),
            pl.BlockSpec((d, tf), lambda i, j: (0, j)),
            pl.BlockSpec((tf, d), lambda i, j: (j, 0)),
            pl.BlockSpec((1, d), lambda i, j: (0, 0)),
        ],
        out_specs=pl.BlockSpec((tm, d), lambda i, j: (i, 0)),
        out_shape=jax.ShapeDtypeStruct((n, d), F32),
        scratch_shapes=[pltpu.VMEM((tm, d), BF16), pltpu.VMEM((tm, d), F32)],
        compiler_params=_params("parallel", "arbitrary"),
        name="ffn",
    )(x, g, w1, w3, w2, gf)


N_IN_SPLIT = 8


K_SLOTS = (1, 6)
V_SLOTS = (2, 7)


def _inproj_kernel(x_ref, g_ref, w_ref, *refs, prompt_layouts):
    outs, extra, h_ref = refs[:N_IN_SPLIT + 1], refs[N_IN_SPLIT + 1:-1], refs[-1]
    j = pl.program_id(1)

    @pl.when(j == 0)
    def _():
        h_ref[...] = _rms(x_ref[...], g_ref[...]).astype(BF16)

    p = jnp.dot(h_ref[...], w_ref[...], preferred_element_type=F32)
    for k in range(N_IN_SPLIT // 2):
        @pl.when(j == k)
        def _(k=k):
            for slot in (2 * k, 2 * k + 1):
                part = p[:, (slot % 2) * ATT_W:(slot % 2 + 1) * ATT_W]
                if not prompt_layouts or slot not in K_SLOTS + V_SLOTS:
                    outs[slot][...] = part
                    continue
                part_t = part.T
                outs[slot][...] = part_t
                if slot in K_SLOTS:
                    kb_ref, kmean_ref = extra[3 * K_SLOTS.index(slot)], extra[3 * K_SLOTS.index(slot) + 1]
                    kb_ref[...] = part.astype(BF16)
                    for blk in range(kmean_ref.shape[0]):
                        kmean_ref[blk:blk + 1, :] = jnp.mean(
                            part[blk * MOBA_BLOCK:(blk + 1) * MOBA_BLOCK, :], axis=0, keepdims=True)
                else:
                    vt_ref = extra[3 * V_SLOTS.index(slot) + 2]
                    for blk in range(vt_ref.shape[0]):
                        vt_ref[blk] = part_t[:, blk * MOBA_BLOCK:(blk + 1) * MOBA_BLOCK].astype(BF16)

    @pl.when(j >= N_IN_SPLIT // 2)
    def _():
        outs[N_IN_SPLIT][...] = p


def _inproj(x, g, w_in, *, tm, seq=None):
    n, d = x.shape
    cols = w_in.shape[1]
    tn = 2 * ATT_W
    n_split_tiles = N_IN_SPLIT // 2
    n_gate_tiles = cols // tn - n_split_tiles
    out_specs = [pl.BlockSpec((tm, ATT_W), lambda i, j: (i, 0)) for _ in range(N_IN_SPLIT)]
    out_specs.append(pl.BlockSpec((tm, tn), lambda i, j: (i, jnp.maximum(j - n_split_tiles, 0))))
    out_shape = [jax.ShapeDtypeStruct((n, ATT_W), F32) for _ in range(N_IN_SPLIT)]
    out_shape.append(jax.ShapeDtypeStruct((n, n_gate_tiles * tn), F32))
    if seq is not None:
        tiles_per_seq = seq // tm
        blocks_per_tile = tm // MOBA_BLOCK
        for slot in K_SLOTS + V_SLOTS:
            out_specs[slot] = pl.BlockSpec((None, ATT_W, tm),
                                           lambda i, j: (i // tiles_per_seq, 0, i % tiles_per_seq))
            out_shape[slot] = jax.ShapeDtypeStruct((n // seq, ATT_W, seq), F32)
        for _ in range(2):
            out_specs.append(pl.BlockSpec((tm, ATT_W), lambda i, j: (i, 0)))
            out_shape.append(jax.ShapeDtypeStruct((n, ATT_W), BF16))
            out_specs.append(pl.BlockSpec((None, blocks_per_tile, ATT_W), lambda i, j: (i, 0, 0)))
            out_shape.append(jax.ShapeDtypeStruct((n // tm, blocks_per_tile, ATT_W), F32))
            out_specs.append(pl.BlockSpec((blocks_per_tile, ATT_W, MOBA_BLOCK), lambda i, j: (i, 0, 0)))
            out_shape.append(jax.ShapeDtypeStruct((n // MOBA_BLOCK, ATT_W, MOBA_BLOCK), BF16))
    return pl.pallas_call(
        functools.partial(_inproj_kernel, prompt_layouts=seq is not None),
        grid=(n // tm, cols // tn),
        in_specs=[
            pl.BlockSpec((tm, d), lambda i, j: (i, 0)),
            pl.BlockSpec((1, d), lambda i, j: (0, 0)),
            pl.BlockSpec((d, tn), lambda i, j: (0, j)),
        ],
        out_specs=out_specs,
        out_shape=out_shape,
        scratch_shapes=[pltpu.VMEM((tm, d), BF16)],
        compiler_params=_params("parallel", "arbitrary"),
        name="inproj",
    )(x, g, w_in)


def _merge_kernel(oa_ref, ob_ref, oc_ref, gl_ref, x_ref, bg_ref, wb_ref, wo_ref, o_ref, *,
                  transposed_attention):
    d = x_ref.shape[1]
    merged = None
    for n, o_n in enumerate((oa_ref, ob_ref, oc_ref)):
        branch = o_n[...]
        if transposed_attention and n != 1:
            branch = branch.T
        proj = jnp.dot(branch.astype(BF16), wb_ref[n], preferred_element_type=F32)
        gate = jax.nn.sigmoid(gl_ref[:, n * d:(n + 1) * d] + bg_ref[n:n + 1, :])
        merged = gate * proj if merged is None else merged + gate * proj
    o_ref[...] = x_ref[...] + jnp.dot(merged.astype(BF16), wo_ref[...], preferred_element_type=F32)


def _merge(oa, ob, oc, gl, x, bg, wb, wo, *, tm, transposed_attention):
    n, d = x.shape
    row = lambda w: pl.BlockSpec((tm, w), lambda i: (i, 0))
    att = pl.BlockSpec((None, ATT_W, tm), lambda i: (i, 0, 0)) if transposed_attention else row(ATT_W)
    return pl.pallas_call(
        functools.partial(_merge_kernel, transposed_attention=transposed_attention),
        grid=(n // tm,),
        in_specs=[att, row(ATT_W), att, row(gl.shape[1]), row(d),
                  pl.BlockSpec(bg.shape, lambda i: (0, 0)),
                  pl.BlockSpec(wb.shape, lambda i: (0, 0, 0)),
                  pl.BlockSpec(wo.shape, lambda i: (0, 0))],
        out_specs=row(d),
        out_shape=jax.ShapeDtypeStruct((n, d), F32),
        compiler_params=_params("parallel"),
        name="merge",
    )(oa, ob, oc, gl, x, bg, wb, wo)


def _rel_bucket(dist):
    max_exact = N_BUCKETS // 2
    d = jnp.maximum(dist, 0)
    df = jnp.maximum(d, 1).astype(F32)
    large = max_exact + (jnp.log(df / max_exact) / math.log(MAX_DIST / max_exact)
                         * (N_BUCKETS - max_exact)).astype(jnp.int32)
    large = jnp.minimum(large, N_BUCKETS - 1)
    return jnp.where(d < max_exact, d, large)


def _bias_tables(rel_bias):
    period = 4 * MOBA_BLOCK
    k = jnp.arange(period, dtype=jnp.int32)
    k = jnp.where(k < period // 2, k, k - period)
    u = rel_bias[_rel_bucket(jnp.maximum(MOBA_BLOCK + k, 0))].T
    n_keys = 2 * MOBA_BLOCK
    toeplitz = jnp.tile(u, (1, n_keys))[:, :n_keys * (period - 1)]
    toeplitz = toeplitz.reshape(N_HEADS, n_keys, period - 1)[:, :, :MOBA_BLOCK]
    heads_per_step = STEP_W // HEAD_DIM
    n_steps = N_HEADS // heads_per_step
    toeplitz = toeplitz.reshape(n_steps, heads_per_step, n_keys, MOBA_BLOCK)
    toeplitz = jnp.transpose(toeplitz, (0, 2, 1, 3)).reshape(n_steps, n_keys, STEP_Q)
    far = rel_bias[_rel_bucket(jnp.int32(2 * MOBA_BLOCK))]
    far_pair = jnp.repeat(far.reshape(n_steps, 1, heads_per_step), MOBA_BLOCK, axis=2)
    dec = rel_bias[_rel_bucket(MOBA_BLOCK - jnp.arange(MOBA_BLOCK, dtype=jnp.int32))].T
    own0 = rel_bias[_rel_bucket(jnp.int32(0))][:, None]
    return toeplitz, far_pair, dec, far[:, None], own0


PAIR_Q = HEADS_PER_TILE * MOBA_BLOCK
TILES_PER_STEP = 4
STEP_W = TILES_PER_STEP * LANES
STEP_Q = TILES_PER_STEP * PAIR_Q


def _stacked_queries(q):
    lane = lax.broadcasted_iota(jnp.int32, (1, LANES), 1)
    stacks = []
    for g in range(TILES_PER_STEP):
        qs = q[:, g * LANES:(g + 1) * LANES] * SCALE
        stacks.append(jnp.concatenate([jnp.where((lane // HEAD_DIM) == hh, qs, 0.0).astype(BF16)
                                       for hh in range(HEADS_PER_TILE)], axis=0))
    return stacks


def _scores(k_blk, qcat):
    return jnp.concatenate([_dot_nt(k_blk[:, g * LANES:(g + 1) * LANES], qcat[g])
                            for g in range(TILES_PER_STEP)], axis=1)


def _values(vt_blk, w):
    return jnp.concatenate([jnp.dot(vt_blk[g * LANES:(g + 1) * LANES, :], w[:, g * PAIR_Q:(g + 1) * PAIR_Q],
                                    preferred_element_type=F32)
                            for g in range(TILES_PER_STEP)], axis=0)


def _per_head_rows(stat):
    return jnp.concatenate([jnp.broadcast_to(stat[:, g * PAIR_Q:(g + 1) * PAIR_Q], (LANES, PAIR_Q))
                            for g in range(TILES_PER_STEP)], axis=0)


def _pair_corners(acc):
    rows = []
    for r in range(STEP_W // HEAD_DIM):
        hh = r % HEADS_PER_TILE
        rows.append(acc[r * HEAD_DIM:(r + 1) * HEAD_DIM, hh * MOBA_BLOCK:(hh + 1) * MOBA_BLOCK])
    return jnp.concatenate(rows, axis=0)


def _moba_prompt_kernel(q_ref, kmean_ref, kb_ref, vt_ref, tb_ref, far_ref, o_ref, sel_ref):
    i = pl.program_id(2)
    nb = kb_ref.shape[0] // MOBA_BLOCK
    q = q_ref[...]
    qcat = _stacked_queries(q)
    lane = lax.broadcasted_iota(jnp.int32, (1, LANES), 1)
    blk_row = lax.broadcasted_iota(jnp.int32, (nb, 1), 0)

    gate = jnp.concatenate(
        [lax.dot_general(jnp.where((lane // HEAD_DIM) == hh, kmean_ref[:, g * LANES:(g + 1) * LANES], 0.0),
                         q[:, g * LANES:(g + 1) * LANES], _NT,
                         precision=lax.Precision.HIGHEST, preferred_element_type=F32)
         for g in range(TILES_PER_STEP) for hh in range(HEADS_PER_TILE)], axis=1)
    gate = jnp.where(blk_row < i, gate, NEG)
    rank = jnp.zeros_like(gate)
    for m in range(nb):
        gm = gate[m:m + 1, :]
        ahead = (gm > gate) | ((gm == gate) & (blk_row > m))
        rank = rank + ahead.astype(F32)
    sel_ref[...] = ((rank < MOBA_TOPK) & (blk_row < i)).astype(F32)

    def block(n, bias, keep, m_run, l_run, acc):
        start = pl.multiple_of(n * MOBA_BLOCK, MOBA_BLOCK)
        s = _scores(kb_ref[pl.ds(start, MOBA_BLOCK), :], qcat) + bias
        s = jnp.where(keep, s, NEG)
        m_new = jnp.max(s, axis=0, keepdims=True)
        if m_run is not None:
            m_new = jnp.maximum(m_run, m_new)
        p = jnp.exp(s - m_new)
        l_new = jnp.sum(p, axis=0, keepdims=True)
        pv = _values(vt_ref[n], p.astype(BF16))
        if m_run is None:
            return m_new, l_new, pv
        alpha = jnp.exp(m_run - m_new)
        return m_new, alpha * l_run + l_new, _per_head_rows(alpha) * acc + pv

    key_i = lax.broadcasted_iota(jnp.int32, (MOBA_BLOCK, STEP_Q), 0)
    qry_i = lax.broadcasted_iota(jnp.int32, (MOBA_BLOCK, STEP_Q), 1) % MOBA_BLOCK
    state0 = block(i, tb_ref[MOBA_BLOCK:, :], key_i <= qry_i, None, None, None)

    def body(n, state):
        bias = jnp.where(n == i - 1, tb_ref[:MOBA_BLOCK, :], far_ref[...])
        return block(n, bias, sel_ref[pl.ds(n, 1), :] > 0.5, *state)

    _, l_fin, acc_fin = lax.fori_loop(0, i, body, state0)
    o_ref[...] = _pair_corners(acc_fin / _per_head_rows(l_fin))


def _moba_prompt(q, k_means, k_bf, vt, toeplitz, far, *, batch, seq):
    nb = seq // MOBA_BLOCK
    return pl.pallas_call(
        _moba_prompt_kernel,
        grid=(batch, ATT_W // STEP_W, nb),
        in_specs=[
            pl.BlockSpec((MOBA_BLOCK, STEP_W), lambda b, p, i: (b * nb + i, p)),
            pl.BlockSpec((nb, STEP_W), lambda b, p, i: (b, p)),
            pl.BlockSpec((seq, STEP_W), lambda b, p, i: (b, p)),
            pl.BlockSpec((nb, STEP_W, MOBA_BLOCK), lambda b, p, i: (b, p, 0)),
            pl.BlockSpec((None, 2 * MOBA_BLOCK, STEP_Q), lambda b, p, i: (p, 0, 0)),
            pl.BlockSpec((None, 1, STEP_Q), lambda b, p, i: (p, 0, 0)),
        ],
        out_specs=pl.BlockSpec((None, STEP_W, MOBA_BLOCK), lambda b, p, i: (b * nb + i, p, 0)),
        out_shape=jax.ShapeDtypeStruct((batch * nb, ATT_W, MOBA_BLOCK), F32),
        scratch_shapes=[pltpu.VMEM((nb, STEP_Q), F32)],
        compiler_params=_params("parallel", "parallel", "arbitrary"),
        name="moba_prompt",
    )(q, k_means, k_bf, vt, toeplitz, far)


SB_BLOCK = 256


def _sb_prompt_kernel(q_ref, kb_ref, vt_ref, o_ref):
    i = pl.program_id(2)
    qcat = _stacked_queries(q_ref[...])
    row_i = lax.broadcasted_iota(jnp.int32, (SB_BLOCK, SB_BLOCK), 0)
    col_i = lax.broadcasted_iota(jnp.int32, (SB_BLOCK, SB_BLOCK), 1)
    later = (row_i < col_i).astype(BF16)

    def block(n, mask, carry):
        start = pl.multiple_of(n * SB_BLOCK, SB_BLOCK)
        z = _scores(kb_ref[pl.ds(start, SB_BLOCK), :], qcat)
        sp = _softplus_abs(z)
        log_keep = -sp if mask is None else jnp.where(mask, -sp, 0.0)
        hi = log_keep.astype(BF16)
        lo = (log_keep - hi.astype(F32)).astype(BF16)
        both = jnp.dot(later, jnp.concatenate([hi, lo], axis=1), preferred_element_type=F32)
        between = both[:, :STEP_Q] + both[:, STEP_Q:]
        if carry is not None:
            between = between + carry
        w = jnp.exp(z - sp + between)
        if mask is not None:
            w = jnp.where(mask, w, 0.0)
        pv = _values(vt_ref[n], w.astype(BF16))
        return jnp.sum(log_keep, axis=0, keepdims=True), pv

    key_i = lax.broadcasted_iota(jnp.int32, (SB_BLOCK, STEP_Q), 0)
    qry_i = lax.broadcasted_iota(jnp.int32, (SB_BLOCK, STEP_Q), 1) % SB_BLOCK
    state0 = block(i, key_i < qry_i, None)

    def body(t, state):
        carry, acc = state
        kept, pv = block(i - 1 - t, None, carry)
        return carry + kept, acc + pv

    _, acc_fin = lax.fori_loop(0, i, body, state0)
    o_ref[...] = _pair_corners(acc_fin)


def _sb_prompt(q, k_bf, vt, *, batch, seq):
    nb = seq // SB_BLOCK
    return pl.pallas_call(
        _sb_prompt_kernel,
        grid=(batch, ATT_W // STEP_W, nb),
        in_specs=[
            pl.BlockSpec((SB_BLOCK, STEP_W), lambda b, p, i: (b * nb + i, p)),
            pl.BlockSpec((seq, STEP_W), lambda b, p, i: (b, p)),
            pl.BlockSpec((nb, STEP_W, SB_BLOCK), lambda b, p, i: (b, p, 0)),
        ],
        out_specs=pl.BlockSpec((None, STEP_W, SB_BLOCK), lambda b, p, i: (b * nb + i, p, 0)),
        out_shape=jax.ShapeDtypeStruct((batch * nb, ATT_W, SB_BLOCK), F32),
        compiler_params=_params("parallel", "parallel", "arbitrary"),
        name="sb_prompt",
    )(q, k_bf, vt)


def _lru_gates(xc, wa_ref, ba_ref, wx_ref, bx_ref, lam_ref):
    xc_bf = xc.astype(BF16)
    r = jax.nn.sigmoid(jnp.dot(xc_bf, wa_ref[...], preferred_element_type=F32) + ba_ref[...])
    gi = jax.nn.sigmoid(jnp.dot(xc_bf, wx_ref[...], preferred_element_type=F32) + bx_ref[...])
    log_a = -RG_C * r * _softplus(-lam_ref[...])
    a = jnp.exp(log_a)
    b = jnp.sqrt(-_expm1(2.0 * log_a)) * (gi * xc)
    return a, b


def _lru_prompt_kernel(x_ref, g_ref, buf_ref, h0_ref, cw_ref, cb_ref, wa_ref, ba_ref, wx_ref, bx_ref,
                       lam_ref, y_ref, hlast_ref, conv_ref, tail_ref, hc_ref):
    t = pl.program_id(1)
    tc = x_ref.shape[0]
    tail_rows = tail_ref.shape[0]

    @pl.when(t == 0)
    def _():
        tail_ref[...] = jnp.zeros_like(tail_ref)
        tail_ref[tail_rows - (CONV_W - 1):, :] = buf_ref[...]
        hc_ref[...] = h0_ref[...]

    x = x_ref[...]
    xe = jnp.concatenate([tail_ref[...], x], axis=0)
    xc = cb_ref[...] + cw_ref[CONV_W - 1:CONV_W, :] * x
    for k in range(1, CONV_W):
        xc = xc + cw_ref[CONV_W - 1 - k:CONV_W - k, :] * pltpu.roll(xe, k, 0)[tail_rows:, :]
    a, b = _lru_gates(xc, wa_ref, ba_ref, wx_ref, bx_ref, lam_ref)

    row = lax.broadcasted_iota(jnp.int32, (tc, 1), 0)
    k = 1
    while k < tc:
        a_sh = jnp.where(row >= k, pltpu.roll(a, k, 0), 1.0)
        b_sh = jnp.where(row >= k, pltpu.roll(b, k, 0), 0.0)
        b = a * b_sh + b
        a = a * a_sh
        k *= 2
    h = a * hc_ref[...] + b
    y_ref[...] = h * jax.nn.gelu(g_ref[...], approximate=True)
    hc_ref[...] = h[tc - 1:tc, :]
    tail_ref[...] = x[tc - tail_rows:, :]

    @pl.when(t == pl.num_programs(1) - 1)
    def _():
        hlast_ref[...] = h[tc - 1:tc, :]
        conv_ref[...] = x[tc - (CONV_W - 1):, :]


def _lru_prompt(x, g, buf, h0, lw, *, batch, seq, tc):
    nt = seq // tc
    w = x.shape[1]
    vec = pl.BlockSpec((1, w), lambda b, t: (0, 0))
    mat = pl.BlockSpec((w, w), lambda b, t: (0, 0))
    return pl.pallas_call(
        _lru_prompt_kernel,
        grid=(batch, nt),
        in_specs=[
            pl.BlockSpec((tc, w), lambda b, t: (b * nt + t, 0)),
            pl.BlockSpec((tc, w), lambda b, t: (b * nt + t, 0)),
            pl.BlockSpec((None, CONV_W - 1, w), lambda b, t: (b, 0, 0)),
            pl.BlockSpec((None, 1, w), lambda b, t: (b, 0, 0)),
            pl.BlockSpec((CONV_W, w), lambda b, t: (0, 0)),
            vec, mat, vec, mat, vec, vec,
        ],
        out_specs=[
            pl.BlockSpec((tc, w), lambda b, t: (b * nt + t, 0)),
            pl.BlockSpec((None, 1, w), lambda b, t: (b, 0, 0)),
            pl.BlockSpec((None, CONV_W - 1, w), lambda b, t: (b, 0, 0)),
        ],
        out_shape=[jax.ShapeDtypeStruct(x.shape, F32),
                   jax.ShapeDtypeStruct((batch, 1, w), F32),
                   jax.ShapeDtypeStruct((batch, CONV_W - 1, w), F32)],
        scratch_shapes=[pltpu.VMEM((8, w), F32), pltpu.VMEM((1, w), F32)],
        compiler_params=_params("parallel", "arbitrary"),
        name="lru_prompt",
    )(x, g, buf, h0, lw["conv_w"], lw["conv_b"], lw["wa"], lw["ba"], lw["wx"], lw["bx"], lw["lam"])


def _lru_step_kernel(x_ref, g_ref, buf_ref, h0_ref, cw_ref, cb_ref, wa_ref, ba_ref, wx_ref, bx_ref,
                     lam_ref, y_ref, h_ref, conv_ref):
    x = x_ref[...]
    xc = cb_ref[...] + cw_ref[CONV_W - 1:CONV_W, :] * x
    for k in range(CONV_W - 1):
        xc = xc + cw_ref[k:k + 1, :] * buf_ref[k]
    a, b = _lru_gates(xc, wa_ref, ba_ref, wx_ref, bx_ref, lam_ref)
    h = a * h0_ref[...] + b
    y_ref[...] = h * jax.nn.gelu(g_ref[...], approximate=True)
    h_ref[...] = h
    for k in range(CONV_W - 2):
        conv_ref[k] = buf_ref[k + 1]
    conv_ref[CONV_W - 2] = x


def _lru_step(x, g, buf, h0, lw):
    n, w = x.shape
    return pl.pallas_call(
        _lru_step_kernel,
        out_shape=[jax.ShapeDtypeStruct((n, w), F32), jax.ShapeDtypeStruct((n, w), F32),
                   jax.ShapeDtypeStruct((CONV_W - 1, n, w), F32)],
        compiler_params=pltpu.CompilerParams(vmem_limit_bytes=VMEM_LIMIT),
        name="lru_step",
    )(x, g, buf, h0, lw["conv_w"], lw["conv_b"], lw["wa"], lw["ba"], lw["wx"], lw["bx"], lw["lam"])


def _page_scores(qb, kt_ref):
    return jnp.concatenate([jnp.sum(qb[h] * kt_ref[h], axis=0, keepdims=True)
                            for h in range(N_HEADS)], axis=0)


def _weighted_values(w, vt_ref):
    return jnp.stack([jnp.broadcast_to(w[h:h + 1, :], (HEAD_DIM, PAGE_SIZE)) * vt_ref[h]
                      for h in range(N_HEADS)], axis=0)


def _sum_positions(acc):
    ones = jnp.ones((N_HEADS, PAGE_SIZE), F32)
    rows = [lax.dot_general(ones, acc[h], _NT, precision=lax.Precision.HIGHEST,
                            preferred_element_type=F32)[:1] for h in range(N_HEADS)]
    return jnp.concatenate(rows, axis=0)


def _moba_decode_kernel(pt_ref, q_ref, q2_ref, kn_ref, vn_ref, dec_ref, far_ref, own_ref, *refs):
    del pt_ref
    n_pg = PAGES_PER_STEP
    k_pages, v_pages = refs[:n_pg], refs[n_pg:2 * n_pg]
    o_ref, gate_ref, m_ref, l_ref, acc_ref = refs[2 * n_pg:]
    c = pl.program_id(1)
    n_steps = pl.num_programs(1)
    blocks_per_step = n_pg // PAGES_PER_MOBA_BLOCK
    qb = jnp.broadcast_to(q_ref[...] * SCALE, (N_HEADS, HEAD_DIM, PAGE_SIZE))
    lane = lax.broadcasted_iota(jnp.int32, (1, LANES), 1)

    @pl.when(c == 0)
    def _():
        gate_ref[...] = jnp.full_like(gate_ref, NEG)
        m_ref[...] = jnp.full_like(m_ref, NEG)
        l_ref[...] = jnp.zeros_like(l_ref)

    for j in range(blocks_per_step):
        n = c * blocks_per_step + j
        pages = range(PAGES_PER_MOBA_BLOCK * j, PAGES_PER_MOBA_BLOCK * (j + 1))
        z = [_page_scores(qb, k_pages[pg]) for pg in pages]
        g = sum(jnp.sum(zt, axis=-1, keepdims=True) for zt in z) * (1.0 / (SCALE * MOBA_BLOCK))
        newest = n == n_steps * blocks_per_step - 1
        s = [zt + jnp.where(newest, dec_ref[:, t * PAGE_SIZE:(t + 1) * PAGE_SIZE], far_ref[...])
             for t, zt in enumerate(z)]
        m = functools.reduce(jnp.maximum, [jnp.max(st, axis=-1, keepdims=True) for st in s])
        p = [jnp.exp(st - m) for st in s]
        l = sum(jnp.sum(pt, axis=-1, keepdims=True) for pt in p)
        acc_ref[n] = sum(_weighted_values(pt, v_pages[pg]) for pt, pg in zip(p, pages))
        gate_ref[...] = jnp.where(lane == n, g, gate_ref[...])
        m_ref[...] = jnp.where(lane == n, m, m_ref[...])
        l_ref[...] = jnp.where(lane == n, l, l_ref[...])

    @pl.when(c == n_steps - 1)
    def _():
        nb = n_steps * blocks_per_step
        gate = gate_ref[...]
        rank = jnp.zeros_like(gate)
        for mm in range(nb):
            gm = gate[:, mm:mm + 1]
            ahead = (gm > gate) | ((gm == gate) & (lane > mm))
            rank = rank + ahead.astype(F32)
        sel = (rank < MOBA_TOPK) & (lane < nb)
        s_own = jnp.sum(q2_ref[...] * SCALE * kn_ref[...], axis=-1, keepdims=True) + own_ref[...]
        m_sel = jnp.where(sel, m_ref[...], NEG)
        m_tot = jnp.maximum(jnp.max(m_sel, axis=-1, keepdims=True), s_own)
        wgt = jnp.where(sel, jnp.exp(m_sel - m_tot), 0.0)
        w_own = jnp.exp(s_own - m_tot)
        l_tot = jnp.sum(wgt * l_ref[...], axis=-1, keepdims=True) + w_own
        merged = []
        for h in range(N_HEADS):
            tot = jnp.zeros((HEAD_DIM, PAGE_SIZE), F32)
            for mm in range(nb):
                tot = tot + jnp.broadcast_to(wgt[h:h + 1, mm:mm + 1], (HEAD_DIM, PAGE_SIZE)) * acc_ref[mm, h]
            merged.append(tot)
        o = w_own * vn_ref[...] + _sum_positions(merged)
        o_ref[...] = o / l_tot


def _page_spec(layer, slot, order):
    def index_map(b, c, pt):
        n_pages = pt.shape[1]
        pos = c * PAGES_PER_STEP + slot
        return (layer, pt[b, pos if order > 0 else n_pages - 1 - pos], 0, 0, 0)
    return pl.BlockSpec((None, None, N_HEADS, HEAD_DIM, PAGE_SIZE), index_map)


def _moba_decode(page_table, q, k_new, v_new, cache_k, cache_v, dec, far, own0, *, layer):
    batch, n_pages = page_table.shape
    n_steps = n_pages // PAGES_PER_STEP
    nb = n_pages // PAGES_PER_MOBA_BLOCK
    heads = pl.BlockSpec((None, N_HEADS, HEAD_DIM), lambda b, c, pt: (b, 0, 0))
    column = pl.BlockSpec((None, N_HEADS, HEAD_DIM, 1), lambda b, c, pt: (b, 0, 0, 0))
    const = lambda a: pl.BlockSpec(a.shape, lambda b, c, pt: (0,) * a.ndim)
    grid_spec = pltpu.PrefetchScalarGridSpec(
        num_scalar_prefetch=1,
        grid=(batch, n_steps),
        in_specs=[column, heads, heads, heads, const(dec), const(far), const(own0)]
        + [_page_spec(layer, s, +1) for s in range(PAGES_PER_STEP)]
        + [_page_spec(layer, s, +1) for s in range(PAGES_PER_STEP)],
        out_specs=heads,
        scratch_shapes=[pltpu.VMEM((N_HEADS, LANES), F32), pltpu.VMEM((N_HEADS, LANES), F32),
                        pltpu.VMEM((N_HEADS, LANES), F32),
                        pltpu.VMEM((nb, N_HEADS, HEAD_DIM, PAGE_SIZE), F32)],
    )
    return pl.pallas_call(
        _moba_decode_kernel,
        grid_spec=grid_spec,
        out_shape=jax.ShapeDtypeStruct((batch, N_HEADS, HEAD_DIM), F32),
        compiler_params=_params("parallel", "arbitrary"),
        name="moba_decode",
    )(page_table, q[..., None], q, k_new, v_new, dec, far, own0,
      *([cache_k] * PAGES_PER_STEP), *([cache_v] * PAGES_PER_STEP))


def _sb_decode_kernel(pt_ref, q_ref, *refs):
    del pt_ref
    n_pg = PAGES_PER_STEP
    k_pages, v_pages = refs[:n_pg], refs[n_pg:2 * n_pg]
    o_ref, carry_ref, acc_ref = refs[2 * n_pg:]
    c = pl.program_id(1)
    qb = jnp.broadcast_to(q_ref[...] * SCALE, (N_HEADS, HEAD_DIM, PAGE_SIZE))
    lane = lax.broadcasted_iota(jnp.int32, (1, PAGE_SIZE), 1)

    @pl.when(c == 0)
    def _():
        carry_ref[...] = jnp.zeros_like(carry_ref)
        acc_ref[...] = jnp.zeros_like(acc_ref)

    carry = carry_ref[...]
    acc = None
    for j in range(n_pg):
        z = _page_scores(qb, k_pages[j])
        sp = _softplus_abs(z)
        log_keep = -sp
        incl = log_keep
        shift = 1
        while shift < PAGE_SIZE:
            moved = pltpu.roll(incl, PAGE_SIZE - shift, 1)
            incl = incl + jnp.where(lane < PAGE_SIZE - shift, moved, 0.0)
            shift *= 2
        w = jnp.exp(z - sp + incl - log_keep + carry)
        wv = _weighted_values(w, v_pages[j])
        acc = wv if acc is None else acc + wv
        carry = carry + jnp.sum(log_keep, axis=-1, keepdims=True)
    carry_ref[...] = carry
    acc_ref[...] += acc

    @pl.when(c == pl.num_programs(1) - 1)
    def _():
        o_ref[...] = _sum_positions(acc_ref[...])


def _sb_decode(page_table, q, cache_k, cache_v, *, layer):
    batch, n_pages = page_table.shape
    n_steps = n_pages // PAGES_PER_STEP
    grid_spec = pltpu.PrefetchScalarGridSpec(
        num_scalar_prefetch=1,
        grid=(batch, n_steps),
        in_specs=[pl.BlockSpec((None, N_HEADS, HEAD_DIM, 1), lambda b, c, pt: (b, 0, 0, 0))]
        + [_page_spec(layer, s, -1) for s in range(PAGES_PER_STEP)]
        + [_page_spec(layer, s, -1) for s in range(PAGES_PER_STEP)],
        out_specs=pl.BlockSpec((None, N_HEADS, HEAD_DIM), lambda b, c, pt: (b, 0, 0)),
        scratch_shapes=[pltpu.VMEM((N_HEADS, 1), F32),
                        pltpu.VMEM((N_HEADS, HEAD_DIM, PAGE_SIZE), F32)],
    )
    return pl.pallas_call(
        _sb_decode_kernel,
        grid_spec=grid_spec,
        out_shape=jax.ShapeDtypeStruct((batch, N_HEADS, HEAD_DIM), F32),
        compiler_params=_params("parallel", "arbitrary"),
        name="sb_decode",
    )(page_table, q[..., None], *([cache_k] * PAGES_PER_STEP), *([cache_v] * PAGES_PER_STEP))


def _block_diag(w):
    nblk, bk, bj = w.shape
    eye = jnp.eye(nblk, dtype=w.dtype)
    return jnp.einsum("nkj,nm->nkmj", w, eye).reshape(nblk * bk, nblk * bj)


def _layer_weights(l, norm_ffn1, ffn1_w1, ffn1_w3, ffn1_w2, norm_mix, w_in, b_gate, conv_w, conv_b,
                   lru_wa, lru_ba, lru_wx, lru_bx, lru_lambda, w_branch, w_out, norm_ffn2,
                   ffn2_w1, ffn2_w3, ffn2_w2):
    vec = lambda a: a[l].reshape(1, -1)
    bf = lambda a: a[l].astype(BF16)
    return {
        "ln1": vec(norm_ffn1), "f1_w1": bf(ffn1_w1), "f1_w3": bf(ffn1_w3), "f1_w2": bf(ffn1_w2),
        "ln_mix": vec(norm_mix), "w_in": bf(w_in), "b_gate": b_gate[l],
        "lru": {"conv_w": conv_w[l], "conv_b": vec(conv_b),
                "wa": _block_diag(lru_wa[l]).astype(BF16), "ba": vec(lru_ba),
                "wx": _block_diag(lru_wx[l]).astype(BF16), "bx": vec(lru_bx), "lam": vec(lru_lambda)},
        "w_branch": bf(w_branch), "w_out": bf(w_out),
        "ln2": vec(norm_ffn2), "f2_w1": bf(ffn2_w1), "f2_w3": bf(ffn2_w3), "f2_w2": bf(ffn2_w2),
    }


FFN_TILE_F = 256


def kernel(x_prompt, x_sample, cache_moba_k, cache_moba_v, cache_sb_k, cache_sb_v, state_lru_h,
           state_lru_conv, page_table, rel_bias, norm_ffn1, ffn1_w1, ffn1_w3, ffn1_w2, norm_mix, w_in,
           b_gate, conv_w, conv_b, lru_wa, lru_ba, lru_wx, lru_bx, lru_lambda, w_branch, w_out,
           norm_ffn2, ffn2_w1, ffn2_w3, ffn2_w2, norm_final):
    bp, seq, d = x_prompt.shape
    db = x_sample.shape[0]
    depth = w_in.shape[0]
    n_phys = cache_moba_k.shape[1]
    toeplitz, far, dec, far_dec, own0 = _bias_tables(rel_bias)
    gf = norm_final.reshape(1, d)
    paged = lambda c: jnp.transpose(c, (0, 1, 3, 4, 2))
    cmk, cmv, csk, csv = paged(cache_moba_k), paged(cache_moba_v), paged(cache_sb_k), paged(cache_sb_v)

    yp = x_prompt.reshape(bp * seq, d)
    ys = x_sample.reshape(db, d)
    tm_p = 1024 if (bp * seq) % 1024 == 0 else MOBA_BLOCK
    zeros_buf = jnp.zeros((bp, CONV_W - 1, LRU_W), F32)
    zeros_h = jnp.zeros((bp, 1, LRU_W), F32)
    outs = {k: [] for k in ("p_mk", "p_mv", "p_sk", "p_sv", "p_h", "p_cv",
                            "s_mk", "s_mv", "s_sk", "s_sv", "s_h", "s_cv")}
    for l in range(depth):
        lw = _layer_weights(l, norm_ffn1, ffn1_w1, ffn1_w3, ffn1_w2, norm_mix, w_in, b_gate, conv_w,
                            conv_b, lru_wa, lru_ba, lru_wx, lru_bx, lru_lambda, w_branch, w_out,
                            norm_ffn2, ffn2_w1, ffn2_w3, ffn2_w2)
        last = l == depth - 1

        yp = _ffn(yp, lw["ln1"], lw["f1_w1"], lw["f1_w3"], lw["f1_w2"], gf,
                  final_norm=False, tm=tm_p, tf=FFN_TILE_F)
        qa, ka, va, xb, gb, qc, kc, vc, gl, ka_bf, ka_means, vat, kc_bf, _, vct = _inproj(
            yp, lw["ln_mix"], lw["w_in"], tm=max(tm_p // 2, MOBA_BLOCK), seq=seq)
        o_a = _moba_prompt(qa, ka_means.reshape(bp * seq // MOBA_BLOCK, ATT_W), ka_bf, vat, toeplitz, far,
                           batch=bp, seq=seq)
        o_b, h_new, conv_new = _lru_prompt(xb, gb, zeros_buf, zeros_h, lw["lru"],
                                           batch=bp, seq=seq, tc=min(seq, 512))
        o_c = _sb_prompt(qc, kc_bf, vct, batch=bp, seq=seq)
        yp = _merge(o_a, o_b, o_c, gl, yp, lw["b_gate"], lw["w_branch"], lw["w_out"], tm=MOBA_BLOCK,
                    transposed_attention=True)
        yp = _ffn(yp, lw["ln2"], lw["f2_w1"], lw["f2_w3"], lw["f2_w2"], gf,
                  final_norm=last, tm=tm_p, tf=FFN_TILE_F)
        outs["p_mk"].append(ka); outs["p_mv"].append(va)
        outs["p_sk"].append(kc); outs["p_sv"].append(vc)
        outs["p_h"].append(h_new.reshape(bp, LRU_W)); outs["p_cv"].append(conv_new)

        ys = _ffn(ys, lw["ln1"], lw["f1_w1"], lw["f1_w3"], lw["f1_w2"], gf,
                  final_norm=False, tm=db, tf=FFN_TILE_F)
        qa, ka, va, xb, gb, qc, kc, vc, gl = _inproj(ys, lw["ln_mix"], lw["w_in"], tm=db)
        by_head = lambda a: a.reshape(db, N_HEADS, HEAD_DIM)
        o_a = _moba_decode(page_table, by_head(qa), by_head(ka), by_head(va), cmk, cmv,
                           dec, far_dec, own0, layer=l).reshape(db, ATT_W)
        o_b, h_new, conv_new = _lru_step(xb, gb, jnp.transpose(state_lru_conv[l], (1, 0, 2)),
                                         state_lru_h[l], lw["lru"])
        o_c = _sb_decode(page_table, by_head(qc), csk, csv, layer=l).reshape(db, ATT_W)
        ys = _merge(o_a, o_b, o_c, gl, ys, lw["b_gate"], lw["w_branch"], lw["w_out"], tm=db,
                    transposed_attention=False)
        ys = _ffn(ys, lw["ln2"], lw["f2_w1"], lw["f2_w3"], lw["f2_w2"], gf,
                  final_norm=last, tm=db, tf=FFN_TILE_F)
        kv_shape = (db, 1, N_HEADS, HEAD_DIM)
        outs["s_mk"].append(ka.reshape(kv_shape)); outs["s_mv"].append(va.reshape(kv_shape))
        outs["s_sk"].append(kc.reshape(kv_shape)); outs["s_sv"].append(vc.reshape(kv_shape))
        outs["s_h"].append(h_new); outs["s_cv"].append(jnp.transpose(conv_new, (1, 0, 2)))

    st = {k: jnp.stack(v) for k, v in outs.items()}
    for k in ("p_mk", "p_mv", "p_sk", "p_sv"):
        st[k] = jnp.transpose(st[k].reshape(depth, bp, N_HEADS, HEAD_DIM, seq), (0, 1, 4, 2, 3))
    return (yp.reshape(bp, seq, d), ys.reshape(db, 1, d),
            st["p_mk"], st["p_mv"], st["p_sk"], st["p_sv"], st["p_h"], st["p_cv"],
            st["s_mk"], st["s_mv"], st["s_sk"], st["s_sv"], st["s_h"], st["s_cv"])
```

```python
import functools
import math

import jax
import jax.numpy as jnp
from jax import lax
from jax.experimental import pallas as pl
from jax.experimental.pallas import tpu as pltpu

F32 = jnp.float32
BF16 = jnp.bfloat16

HEAD_DIM = 64
N_HEADS = 8
ATT_W = N_HEADS * HEAD_DIM
LANES = 128
HEADS_PER_TILE = LANES // HEAD_DIM
N_PAIRS = ATT_W // LANES
MOBA_BLOCK = 256
MOBA_TOPK = 3
PAGE_SIZE = 128
PAGES_PER_MOBA_BLOCK = MOBA_BLOCK // PAGE_SIZE
PAGES_PER_STEP = 16
LRU_W = 512
LRU_BLOCKS = 8
CONV_W = 4
RG_C = 8.0
N_BUCKETS = 32
MAX_DIST = 128
RMS_EPS = 1e-6
NEG = -1e30
SCALE = HEAD_DIM ** -0.5
VMEM_LIMIT = 56 * 1024 * 1024

_NT = (((1,), (1,)), ((), ()))


def _params(*sem):
    return pltpu.CompilerParams(dimension_semantics=sem, vmem_limit_bytes=VMEM_LIMIT)


def _rms(x, g):
    return x * lax.rsqrt(jnp.mean(x * x, axis=-1, keepdims=True) + RMS_EPS) * g


def _softplus(z):
    return jnp.maximum(z, 0.0) + jnp.log1p(jnp.exp(-jnp.abs(z)))


def _softplus_abs(z):
    return jnp.maximum(z, 0.0) + jnp.log(1.0 + jnp.exp(-jnp.abs(z)))


def _expm1(x):
    u = jnp.exp(x)
    um1 = u - 1.0
    return jnp.where(um1 == 0.0, x, jnp.where(um1 == -1.0, -1.0, um1 * x / jnp.log(u)))


def _dot_nt(a, b):
    return lax.dot_general(a.astype(BF16), b.astype(BF16), _NT, preferred_element_type=F32)


def _ffn_kernel(x_ref, g_ref, w1_ref, w3_ref, w2_ref, gf_ref, o_ref, h_ref, acc_ref, *, final_norm):
    j = pl.program_id(1)

    @pl.when(j == 0)
    def _():
        h_ref[...] = _rms(x_ref[...], g_ref[...]).astype(BF16)
        acc_ref[...] = jnp.zeros_like(acc_ref)

    h = h_ref[...]
    a = jnp.dot(h, w1_ref[...], preferred_element_type=F32)
    b = jnp.dot(h, w3_ref[...], preferred_element_type=F32)
    u = a * jax.nn.sigmoid(a) * b
    acc_ref[...] += jnp.dot(u.astype(BF16), w2_ref[...], preferred_element_type=F32)

    @pl.when(j == pl.num_programs(1) - 1)
    def _():
        y = x_ref[...] + 0.5 * acc_ref[...]
        if final_norm:
            y = _rms(y, gf_ref[...])
        o_ref[...] = y


def _ffn(x, g, w1, w3, w2, gf, *, layer, final_norm, tm, tf):
    n, d = x.shape
    dff = w1.shape[2]
    return pl.pallas_call(
        functools.partial(_ffn_kernel, final_norm=final_norm),
        grid=(n // tm, dff // tf),
        in_specs=[
            pl.BlockSpec((tm, d), lambda i, j: (i, 0)),
            pl.BlockSpec((1, d), lambda i, j: (0, 0)),
            pl.BlockSpec((None, d, tf), lambda i, j: (layer, 0, j)),
            pl.BlockSpec((None, d, tf), lambda i, j: (layer, 0, j)),
            pl.BlockSpec((None, tf, d), lambda i, j: (layer, j, 0)),
            pl.BlockSpec((1, d), lambda i, j: (0, 0)),
        ],
        out_specs=pl.BlockSpec((tm, d), lambda i, j: (i, 0)),
        out_shape=jax.ShapeDtypeStruct((n, d), F32),
        scratch_shapes=[pltpu.VMEM((tm, d), BF16), pltpu.VMEM((tm, d), F32)],
        compiler_params=_params("parallel", "arbitrary"),
        name="ffn",
    )(x, g, w1, w3, w2, gf)


N_IN_SPLIT = 8
K_SLOTS = (1, 6)
V_SLOTS = (2, 7)


def _inproj_kernel(x_ref, g_ref, w_ref, *refs, prompt_layouts):
    outs, extra, h_ref = refs[:N_IN_SPLIT], refs[N_IN_SPLIT:-1], refs[-1]
    j = pl.program_id(1)

    @pl.when(j == 0)
    def _():
        h_ref[...] = _rms(x_ref[...], g_ref[...]).astype(BF16)

    for k in range(N_IN_SPLIT // 2):
        @pl.when(j == k)
        def _(k=k):
            p = jnp.dot(h_ref[...], w_ref[...], preferred_element_type=F32)
            for slot in (2 * k, 2 * k + 1):
                part = p[:, (slot % 2) * ATT_W:(slot % 2 + 1) * ATT_W]
                if not prompt_layouts or slot not in K_SLOTS + V_SLOTS:
                    outs[slot][...] = part
                    continue
                part_t = part.T
                outs[slot][...] = part_t
                if slot in K_SLOTS:
                    kb_ref, kmean_ref = extra[3 * K_SLOTS.index(slot)], extra[3 * K_SLOTS.index(slot) + 1]
                    kb_ref[...] = part.astype(BF16)
                    for blk in range(kmean_ref.shape[0]):
                        kmean_ref[blk:blk + 1, :] = jnp.mean(
                            part[blk * MOBA_BLOCK:(blk + 1) * MOBA_BLOCK, :], axis=0, keepdims=True)
                else:
                    vt_ref = extra[3 * V_SLOTS.index(slot) + 2]
                    for blk in range(vt_ref.shape[0]):
                        vt_ref[blk] = part_t[:, blk * MOBA_BLOCK:(blk + 1) * MOBA_BLOCK].astype(BF16)


def _inproj(x, g, w_in, *, layer, tm, seq=None):
    n, d = x.shape
    tn = 2 * ATT_W
    out_specs = [pl.BlockSpec((tm, ATT_W), lambda i, j: (i, 0)) for _ in range(N_IN_SPLIT)]
    out_shape = [jax.ShapeDtypeStruct((n, ATT_W), F32) for _ in range(N_IN_SPLIT)]
    if seq is not None:
        tiles_per_seq = seq // tm
        blocks_per_tile = tm // MOBA_BLOCK
        for slot in K_SLOTS + V_SLOTS:
            out_specs[slot] = pl.BlockSpec((None, ATT_W, tm),
                                           lambda i, j: (i // tiles_per_seq, 0, i % tiles_per_seq))
            out_shape[slot] = jax.ShapeDtypeStruct((n // seq, ATT_W, seq), F32)
        for _ in range(2):
            out_specs.append(pl.BlockSpec((tm, ATT_W), lambda i, j: (i, 0)))
            out_shape.append(jax.ShapeDtypeStruct((n, ATT_W), BF16))
            out_specs.append(pl.BlockSpec((None, blocks_per_tile, ATT_W), lambda i, j: (i, 0, 0)))
            out_shape.append(jax.ShapeDtypeStruct((n // tm, blocks_per_tile, ATT_W), F32))
            out_specs.append(pl.BlockSpec((blocks_per_tile, ATT_W, MOBA_BLOCK), lambda i, j: (i, 0, 0)))
            out_shape.append(jax.ShapeDtypeStruct((n // MOBA_BLOCK, ATT_W, MOBA_BLOCK), BF16))
    return pl.pallas_call(
        functools.partial(_inproj_kernel, prompt_layouts=seq is not None),
        grid=(n // tm, N_IN_SPLIT * ATT_W // tn),
        in_specs=[
            pl.BlockSpec((tm, d), lambda i, j: (i, 0)),
            pl.BlockSpec((1, d), lambda i, j: (0, 0)),
            pl.BlockSpec((None, d, tn), lambda i, j: (layer, 0, j)),
        ],
        out_specs=out_specs,
        out_shape=out_shape,
        scratch_shapes=[pltpu.VMEM((tm, d), BF16)],
        compiler_params=_params("parallel", "arbitrary"),
        name="inproj",
    )(x, g, w_in)


N_BRANCH = 3


def _merge_kernel(oa_ref, ob_ref, oc_ref, x_ref, g_ref, bg_ref, *refs, transposed_attention):
    wg_refs, (wb_ref, wo_ref, o_ref) = refs[:N_BRANCH], refs[N_BRANCH:]
    x = x_ref[...]
    h = _rms(x, g_ref[...]).astype(BF16)
    merged = None
    for n, o_n in enumerate((oa_ref, ob_ref, oc_ref)):
        branch = o_n[...]
        if transposed_attention and n != 1:
            branch = branch.T
        proj = jnp.dot(branch.astype(BF16), wb_ref[n], preferred_element_type=F32)
        logits = jnp.dot(h, wg_refs[n][...], preferred_element_type=F32)
        gate = jax.nn.sigmoid(logits + bg_ref[n:n + 1, :])
        merged = gate * proj if merged is None else merged + gate * proj
    o_ref[...] = x + jnp.dot(merged.astype(BF16), wo_ref[...], preferred_element_type=F32)


def _merge(oa, ob, oc, x, g, bg, w_in, wb, wo, *, layer, tm, transposed_attention):
    n, d = x.shape
    first_gate_tile = w_in.shape[2] // d - N_BRANCH
    row = lambda w: pl.BlockSpec((tm, w), lambda i: (i, 0))
    att = pl.BlockSpec((None, ATT_W, tm), lambda i: (i, 0, 0)) if transposed_attention else row(ATT_W)
    gate_w = [pl.BlockSpec((None, d, d), lambda i, t=first_gate_tile + k: (layer, 0, t))
              for k in range(N_BRANCH)]
    return pl.pallas_call(
        functools.partial(_merge_kernel, transposed_attention=transposed_attention),
        grid=(n // tm,),
        in_specs=[att, row(ATT_W), att, row(d),
                  pl.BlockSpec((1, d), lambda i: (0, 0)),
                  pl.BlockSpec(bg.shape, lambda i: (0, 0)),
                  *gate_w,
                  pl.BlockSpec((None,) + wb.shape[1:], lambda i: (layer, 0, 0, 0)),
                  pl.BlockSpec((None,) + wo.shape[1:], lambda i: (layer, 0, 0))],
        out_specs=row(d),
        out_shape=jax.ShapeDtypeStruct((n, d), F32),
        compiler_params=_params("parallel"),
        name="merge",
    )(oa, ob, oc, x, g, bg, *([w_in] * N_BRANCH), wb, wo)


def _rel_bucket(dist):
    max_exact = N_BUCKETS // 2
    d = jnp.maximum(dist, 0)
    df = jnp.maximum(d, 1).astype(F32)
    large = max_exact + (jnp.log(df / max_exact) / math.log(MAX_DIST / max_exact)
                         * (N_BUCKETS - max_exact)).astype(jnp.int32)
    large = jnp.minimum(large, N_BUCKETS - 1)
    return jnp.where(d < max_exact, d, large)


def _bias_tables(rel_bias):
    period = 4 * MOBA_BLOCK
    k = jnp.arange(period, dtype=jnp.int32)
    k = jnp.where(k < period // 2, k, k - period)
    u = rel_bias[_rel_bucket(jnp.maximum(MOBA_BLOCK + k, 0))].T
    n_keys = 2 * MOBA_BLOCK
    toeplitz = jnp.tile(u, (1, n_keys))[:, :n_keys * (period - 1)]
    toeplitz = toeplitz.reshape(N_HEADS, n_keys, period - 1)[:, :, :MOBA_BLOCK]
    heads_per_step = STEP_W // HEAD_DIM
    n_steps = N_HEADS // heads_per_step
    toeplitz = toeplitz.reshape(n_steps, heads_per_step, n_keys, MOBA_BLOCK)
    toeplitz = jnp.transpose(toeplitz, (0, 2, 1, 3)).reshape(n_steps, n_keys, STEP_Q)
    far = rel_bias[_rel_bucket(jnp.int32(2 * MOBA_BLOCK))]
    far_pair = jnp.repeat(far.reshape(n_steps, 1, heads_per_step), MOBA_BLOCK, axis=2)
    dec = rel_bias[_rel_bucket(MOBA_BLOCK - jnp.arange(MOBA_BLOCK, dtype=jnp.int32))].T
    own0 = rel_bias[_rel_bucket(jnp.int32(0))][:, None]
    return toeplitz, far_pair, dec, far[:, None], own0


PAIR_Q = HEADS_PER_TILE * MOBA_BLOCK
TILES_PER_STEP = 4
STEP_W = TILES_PER_STEP * LANES
STEP_Q = TILES_PER_STEP * PAIR_Q


def _stacked_queries(q):
    lane = lax.broadcasted_iota(jnp.int32, (1, LANES), 1)
    stacks = []
    for g in range(TILES_PER_STEP):
        qs = q[:, g * LANES:(g + 1) * LANES] * SCALE
        stacks.append(jnp.concatenate([jnp.where((lane // HEAD_DIM) == hh, qs, 0.0).astype(BF16)
                                       for hh in range(HEADS_PER_TILE)], axis=0))
    return stacks


def _scores(k_blk, qcat):
    return jnp.concatenate([_dot_nt(k_blk[:, g * LANES:(g + 1) * LANES], qcat[g])
                            for g in range(TILES_PER_STEP)], axis=1)


def _values(vt_blk, w):
    return jnp.concatenate([jnp.dot(vt_blk[g * LANES:(g + 1) * LANES, :], w[:, g * PAIR_Q:(g + 1) * PAIR_Q],
                                    preferred_element_type=F32)
                            for g in range(TILES_PER_STEP)], axis=0)


def _per_head_rows(stat):
    return jnp.concatenate([jnp.broadcast_to(stat[:, g * PAIR_Q:(g + 1) * PAIR_Q], (LANES, PAIR_Q))
                            for g in range(TILES_PER_STEP)], axis=0)


def _pair_corners(acc):
    rows = []
    for r in range(STEP_W // HEAD_DIM):
        hh = r % HEADS_PER_TILE
        rows.append(acc[r * HEAD_DIM:(r + 1) * HEAD_DIM, hh * MOBA_BLOCK:(hh + 1) * MOBA_BLOCK])
    return jnp.concatenate(rows, axis=0)


def _moba_prompt_kernel(q_ref, kmean_ref, kb_ref, vt_ref, tb_ref, far_ref, o_ref, sel_ref):
    i = pl.program_id(2)
    nb = kb_ref.shape[0] // MOBA_BLOCK
    q = q_ref[...]
    qcat = _stacked_queries(q)
    lane = lax.broadcasted_iota(jnp.int32, (1, LANES), 1)
    blk_row = lax.broadcasted_iota(jnp.int32, (nb, 1), 0)

    gate = jnp.concatenate(
        [lax.dot_general(jnp.where((lane // HEAD_DIM) == hh, kmean_ref[:, g * LANES:(g + 1) * LANES], 0.0),
                         q[:, g * LANES:(g + 1) * LANES], _NT,
                         precision=lax.Precision.HIGHEST, preferred_element_type=F32)
         for g in range(TILES_PER_STEP) for hh in range(HEADS_PER_TILE)], axis=1)
    gate = jnp.where(blk_row < i, gate, NEG)
    rank = jnp.zeros_like(gate)
    for m in range(nb):
        gm = gate[m:m + 1, :]
        ahead = (gm > gate) | ((gm == gate) & (blk_row > m))
        rank = rank + ahead.astype(F32)
    sel_ref[...] = ((rank < MOBA_TOPK) & (blk_row < i)).astype(F32)

    def block(n, bias, keep, m_run, l_run, acc):
        start = pl.multiple_of(n * MOBA_BLOCK, MOBA_BLOCK)
        s = _scores(kb_ref[pl.ds(start, MOBA_BLOCK), :], qcat) + bias
        s = jnp.where(keep, s, NEG)
        m_new = jnp.max(s, axis=0, keepdims=True)
        if m_run is not None:
            m_new = jnp.maximum(m_run, m_new)
        p = jnp.exp(s - m_new)
        l_new = jnp.sum(p, axis=0, keepdims=True)
        pv = _values(vt_ref[n], p.astype(BF16))
        if m_run is None:
            return m_new, l_new, pv
        alpha = jnp.exp(m_run - m_new)
        return m_new, alpha * l_run + l_new, _per_head_rows(alpha) * acc + pv

    key_i = lax.broadcasted_iota(jnp.int32, (MOBA_BLOCK, STEP_Q), 0)
    qry_i = lax.broadcasted_iota(jnp.int32, (MOBA_BLOCK, STEP_Q), 1) % MOBA_BLOCK
    state0 = block(i, tb_ref[MOBA_BLOCK:, :], key_i <= qry_i, None, None, None)

    def body(n, state):
        bias = jnp.where(n == i - 1, tb_ref[:MOBA_BLOCK, :], far_ref[...])
        return block(n, bias, sel_ref[pl.ds(n, 1), :] > 0.5, *state)

    _, l_fin, acc_fin = lax.fori_loop(0, i, body, state0)
    o_ref[...] = _pair_corners(acc_fin / _per_head_rows(l_fin))


def _moba_prompt(q, k_means, k_bf, vt, toeplitz, far, *, batch, seq):
    nb = seq // MOBA_BLOCK
    return pl.pallas_call(
        _moba_prompt_kernel,
        grid=(batch, ATT_W // STEP_W, nb),
        in_specs=[
            pl.BlockSpec((MOBA_BLOCK, STEP_W), lambda b, p, i: (b * nb + i, p)),
            pl.BlockSpec((nb, STEP_W), lambda b, p, i: (b, p)),
            pl.BlockSpec((seq, STEP_W), lambda b, p, i: (b, p)),
            pl.BlockSpec((nb, STEP_W, MOBA_BLOCK), lambda b, p, i: (b, p, 0)),
            pl.BlockSpec((None, 2 * MOBA_BLOCK, STEP_Q), lambda b, p, i: (p, 0, 0)),
            pl.BlockSpec((None, 1, STEP_Q), lambda b, p, i: (p, 0, 0)),
        ],
        out_specs=pl.BlockSpec((None, STEP_W, MOBA_BLOCK), lambda b, p, i: (b * nb + i, p, 0)),
        out_shape=jax.ShapeDtypeStruct((batch * nb, ATT_W, MOBA_BLOCK), F32),
        scratch_shapes=[pltpu.VMEM((nb, STEP_Q), F32)],
        compiler_params=_params("parallel", "parallel", "arbitrary"),
        name="moba_prompt",
    )(q, k_means, k_bf, vt, toeplitz, far)


SB_BLOCK = 256


def _sb_prompt_kernel(q_ref, kb_ref, vt_ref, o_ref):
    i = pl.program_id(2)
    qcat = _stacked_queries(q_ref[...])
    row_i = lax.broadcasted_iota(jnp.int32, (SB_BLOCK, SB_BLOCK), 0)
    col_i = lax.broadcasted_iota(jnp.int32, (SB_BLOCK, SB_BLOCK), 1)
    later = (row_i < col_i).astype(BF16)

    def block(n, mask, carry):
        start = pl.multiple_of(n * SB_BLOCK, SB_BLOCK)
        z = _scores(kb_ref[pl.ds(start, SB_BLOCK), :], qcat)
        sp = _softplus_abs(z)
        log_keep = -sp if mask is None else jnp.where(mask, -sp, 0.0)
        hi = log_keep.astype(BF16)
        lo = (log_keep - hi.astype(F32)).astype(BF16)
        both = jnp.dot(later, jnp.concatenate([hi, lo], axis=1), preferred_element_type=F32)
        between = both[:, :STEP_Q] + both[:, STEP_Q:]
        if carry is not None:
            between = between + carry
        w = jnp.exp(z - sp + between)
        if mask is not None:
            w = jnp.where(mask, w, 0.0)
        pv = _values(vt_ref[n], w.astype(BF16))
        return jnp.sum(log_keep, axis=0, keepdims=True), pv

    key_i = lax.broadcasted_iota(jnp.int32, (SB_BLOCK, STEP_Q), 0)
    qry_i = lax.broadcasted_iota(jnp.int32, (SB_BLOCK, STEP_Q), 1) % SB_BLOCK
    state0 = block(i, key_i < qry_i, None)

    def body(t, state):
        carry, acc = state
        kept, pv = block(i - 1 - t, None, carry)
        return carry + kept, acc + pv

    _, acc_fin = lax.fori_loop(0, i, body, state0)
    o_ref[...] = _pair_corners(acc_fin)


def _sb_prompt(q, k_bf, vt, *, batch, seq):
    nb = seq // SB_BLOCK
    return pl.pallas_call(
        _sb_prompt_kernel,
        grid=(batch, ATT_W // STEP_W, nb),
        in_specs=[
            pl.BlockSpec((SB_BLOCK, STEP_W), lambda b, p, i: (b * nb + i, p)),
            pl.BlockSpec((seq, STEP_W), lambda b, p, i: (b, p)),
            pl.BlockSpec((nb, STEP_W, SB_BLOCK), lambda b, p, i: (b, p, 0)),
        ],
        out_specs=pl.BlockSpec((None, STEP_W, SB_BLOCK), lambda b, p, i: (b * nb + i, p, 0)),
        out_shape=jax.ShapeDtypeStruct((batch * nb, ATT_W, SB_BLOCK), F32),
        compiler_params=_params("parallel", "parallel", "arbitrary"),
        name="sb_prompt",
    )(q, k_bf, vt)


def _lru_gates(xc, wa_ref, ba_ref, wx_ref, bx_ref, lam_ref):
    xc_bf = xc.astype(BF16)
    r = jax.nn.sigmoid(jnp.dot(xc_bf, wa_ref[...], preferred_element_type=F32) + ba_ref[...])
    gi = jax.nn.sigmoid(jnp.dot(xc_bf, wx_ref[...], preferred_element_type=F32) + bx_ref[...])
    log_a = -RG_C * r * _softplus(-lam_ref[...])
    a = jnp.exp(log_a)
    b = jnp.sqrt(-_expm1(2.0 * log_a)) * (gi * xc)
    return a, b


def _lru_prompt_kernel(x_ref, g_ref, buf_ref, h0_ref, cw_ref, cb_ref, wa_ref, ba_ref, wx_ref, bx_ref,
                       lam_ref, y_ref, hlast_ref, conv_ref, tail_ref, hc_ref):
    t = pl.program_id(1)
    tc = x_ref.shape[0]
    tail_rows = tail_ref.shape[0]

    @pl.when(t == 0)
    def _():
        tail_ref[...] = jnp.zeros_like(tail_ref)
        tail_ref[tail_rows - (CONV_W - 1):, :] = buf_ref[...]
        hc_ref[...] = h0_ref[...]

    x = x_ref[...]
    xe = jnp.concatenate([tail_ref[...], x], axis=0)
    xc = cb_ref[...] + cw_ref[CONV_W - 1:CONV_W, :] * x
    for k in range(1, CONV_W):
        xc = xc + cw_ref[CONV_W - 1 - k:CONV_W - k, :] * pltpu.roll(xe, k, 0)[tail_rows:, :]
    a, b = _lru_gates(xc, wa_ref, ba_ref, wx_ref, bx_ref, lam_ref)

    row = lax.broadcasted_iota(jnp.int32, (tc, 1), 0)
    k = 1
    while k < tc:
        a_sh = jnp.where(row >= k, pltpu.roll(a, k, 0), 1.0)
        b_sh = jnp.where(row >= k, pltpu.roll(b, k, 0), 0.0)
        b = a * b_sh + b
        a = a * a_sh
        k *= 2
    h = a * hc_ref[...] + b
    y_ref[...] = h * jax.nn.gelu(g_ref[...], approximate=True)
    hc_ref[...] = h[tc - 1:tc, :]
    tail_ref[...] = x[tc - tail_rows:, :]

    @pl.when(t == pl.num_programs(1) - 1)
    def _():
        hlast_ref[...] = h[tc - 1:tc, :]
        conv_ref[...] = x[tc - (CONV_W - 1):, :]


def _lru_prompt(x, g, buf, h0, lw, *, batch, seq, tc):
    nt = seq // tc
    w = x.shape[1]
    vec = pl.BlockSpec((1, w), lambda b, t: (0, 0))
    mat = pl.BlockSpec((w, w), lambda b, t: (0, 0))
    return pl.pallas_call(
        _lru_prompt_kernel,
        grid=(batch, nt),
        in_specs=[
            pl.BlockSpec((tc, w), lambda b, t: (b * nt + t, 0)),
            pl.BlockSpec((tc, w), lambda b, t: (b * nt + t, 0)),
            pl.BlockSpec((None, CONV_W - 1, w), lambda b, t: (b, 0, 0)),
            pl.BlockSpec((None, 1, w), lambda b, t: (b, 0, 0)),
            pl.BlockSpec((CONV_W, w), lambda b, t: (0, 0)),
            vec, mat, vec, mat, vec, vec,
        ],
        out_specs=[
            pl.BlockSpec((tc, w), lambda b, t: (b * nt + t, 0)),
            pl.BlockSpec((None, 1, w), lambda b, t: (b, 0, 0)),
            pl.BlockSpec((None, CONV_W - 1, w), lambda b, t: (b, 0, 0)),
        ],
        out_shape=[jax.ShapeDtypeStruct(x.shape, F32),
                   jax.ShapeDtypeStruct((batch, 1, w), F32),
                   jax.ShapeDtypeStruct((batch, CONV_W - 1, w), F32)],
        scratch_shapes=[pltpu.VMEM((8, w), F32), pltpu.VMEM((1, w), F32)],
        compiler_params=_params("parallel", "arbitrary"),
        name="lru_prompt",
    )(x, g, buf, h0, lw["conv_w"], lw["conv_b"], lw["wa"], lw["ba"], lw["wx"], lw["bx"], lw["lam"])


def _lru_step_kernel(x_ref, g_ref, buf_ref, h0_ref, cw_ref, cb_ref, wa_ref, ba_ref, wx_ref, bx_ref,
                     lam_ref, y_ref, h_ref, conv_ref):
    x = x_ref[...]
    xc = cb_ref[...] + cw_ref[CONV_W - 1:CONV_W, :] * x
    for k in range(CONV_W - 1):
        xc = xc + cw_ref[k:k + 1, :] * buf_ref[k]
    a, b = _lru_gates(xc, wa_ref, ba_ref, wx_ref, bx_ref, lam_ref)
    h = a * h0_ref[...] + b
    y_ref[...] = h * jax.nn.gelu(g_ref[...], approximate=True)
    h_ref[...] = h
    for k in range(CONV_W - 2):
        conv_ref[k] = buf_ref[k + 1]
    conv_ref[CONV_W - 2] = x


def _lru_step(x, g, buf, h0, lw):
    n, w = x.shape
    return pl.pallas_call(
        _lru_step_kernel,
        out_shape=[jax.ShapeDtypeStruct((n, w), F32), jax.ShapeDtypeStruct((n, w), F32),
                   jax.ShapeDtypeStruct((CONV_W - 1, n, w), F32)],
        compiler_params=pltpu.CompilerParams(vmem_limit_bytes=VMEM_LIMIT),
        name="lru_step",
    )(x, g, buf, h0, lw["conv_w"], lw["conv_b"], lw["wa"], lw["ba"], lw["wx"], lw["bx"], lw["lam"])


def _page_scores(qb, kt_ref):
    return jnp.concatenate([jnp.sum(qb[h] * kt_ref[h], axis=0, keepdims=True)
                            for h in range(N_HEADS)], axis=0)


def _weighted_values(w, vt_ref):
    return jnp.stack([jnp.broadcast_to(w[h:h + 1, :], (HEAD_DIM, PAGE_SIZE)) * vt_ref[h]
                      for h in range(N_HEADS)], axis=0)


def _sum_positions(acc):
    ones = jnp.ones((N_HEADS, PAGE_SIZE), F32)
    rows = [lax.dot_general(ones, acc[h], _NT, precision=lax.Precision.HIGHEST,
                            preferred_element_type=F32)[:1] for h in range(N_HEADS)]
    return jnp.concatenate(rows, axis=0)


def _moba_decode_kernel(pt_ref, q_ref, q2_ref, kn_ref, vn_ref, dec_ref, far_ref, own_ref, *refs):
    del pt_ref
    n_pg = PAGES_PER_STEP
    k_pages, v_pages = refs[:n_pg], refs[n_pg:2 * n_pg]
    o_ref, gate_ref, m_ref, l_ref, acc_ref = refs[2 * n_pg:]
    c = pl.program_id(1)
    n_steps = pl.num_programs(1)
    blocks_per_step = n_pg // PAGES_PER_MOBA_BLOCK
    qb = jnp.broadcast_to(q_ref[...] * SCALE, (N_HEADS, HEAD_DIM, PAGE_SIZE))
    lane = lax.broadcasted_iota(jnp.int32, (1, LANES), 1)

    @pl.when(c == 0)
    def _():
        gate_ref[...] = jnp.full_like(gate_ref, NEG)
        m_ref[...] = jnp.full_like(m_ref, NEG)
        l_ref[...] = jnp.zeros_like(l_ref)

    for j in range(blocks_per_step):
        n = c * blocks_per_step + j
        pages = range(PAGES_PER_MOBA_BLOCK * j, PAGES_PER_MOBA_BLOCK * (j + 1))
        z = [_page_scores(qb, k_pages[pg]) for pg in pages]
        g = sum(jnp.sum(zt, axis=-1, keepdims=True) for zt in z) * (1.0 / (SCALE * MOBA_BLOCK))
        newest = n == n_steps * blocks_per_step - 1
        s = [zt + jnp.where(newest, dec_ref[:, t * PAGE_SIZE:(t + 1) * PAGE_SIZE], far_ref[...])
             for t, zt in enumerate(z)]
        m = functools.reduce(jnp.maximum, [jnp.max(st, axis=-1, keepdims=True) for st in s])
        p = [jnp.exp(st - m) for st in s]
        l = sum(jnp.sum(pt, axis=-1, keepdims=True) for pt in p)
        acc_ref[n] = sum(_weighted_values(pt, v_pages[pg]) for pt, pg in zip(p, pages))
        gate_ref[...] = jnp.where(lane == n, g, gate_ref[...])
        m_ref[...] = jnp.where(lane == n, m, m_ref[...])
        l_ref[...] = jnp.where(lane == n, l, l_ref[...])

    @pl.when(c == n_steps - 1)
    def _():
        nb = n_steps * blocks_per_step
        gate = gate_ref[...]
        rank = jnp.zeros_like(gate)
        for mm in range(nb):
            gm = gate[:, mm:mm + 1]
            ahead = (gm > gate) | ((gm == gate) & (lane > mm))
            rank = rank + ahead.astype(F32)
        sel = (rank < MOBA_TOPK) & (lane < nb)
        s_own = jnp.sum(q2_ref[...] * SCALE * kn_ref[...], axis=-1, keepdims=True) + own_ref[...]
        m_sel = jnp.where(sel, m_ref[...], NEG)
        m_tot = jnp.maximum(jnp.max(m_sel, axis=-1, keepdims=True), s_own)
        wgt = jnp.where(sel, jnp.exp(m_sel - m_tot), 0.0)
        w_own = jnp.exp(s_own - m_tot)
        l_tot = jnp.sum(wgt * l_ref[...], axis=-1, keepdims=True) + w_own
        merged = []
        for h in range(N_HEADS):
            tot = jnp.zeros((HEAD_DIM, PAGE_SIZE), F32)
            for mm in range(nb):
                tot = tot + jnp.broadcast_to(wgt[h:h + 1, mm:mm + 1], (HEAD_DIM, PAGE_SIZE)) * acc_ref[mm, h]
            merged.append(tot)
        o = w_own * vn_ref[...] + _sum_positions(merged)
        o_ref[...] = o / l_tot


def _page_spec(layer, slot, order):
    def index_map(b, c, pt):
        n_pages = pt.shape[1]
        pos = c * PAGES_PER_STEP + slot
        return (layer, pt[b, pos if order > 0 else n_pages - 1 - pos], 0, 0, 0)
    return pl.BlockSpec((None, None, N_HEADS, HEAD_DIM, PAGE_SIZE), index_map)


def _moba_decode(page_table, q, k_new, v_new, cache_k, cache_v, dec, far, own0, *, layer):
    batch, n_pages = page_table.shape
    n_steps = n_pages // PAGES_PER_STEP
    nb = n_pages // PAGES_PER_MOBA_BLOCK
    heads = pl.BlockSpec((None, N_HEADS, HEAD_DIM), lambda b, c, pt: (b, 0, 0))
    column = pl.BlockSpec((None, N_HEADS, HEAD_DIM, 1), lambda b, c, pt: (b, 0, 0, 0))
    const = lambda a: pl.BlockSpec(a.shape, lambda b, c, pt: (0,) * a.ndim)
    grid_spec = pltpu.PrefetchScalarGridSpec(
        num_scalar_prefetch=1,
        grid=(batch, n_steps),
        in_specs=[column, heads, heads, heads, const(dec), const(far), const(own0)]
        + [_page_spec(layer, s, +1) for s in range(PAGES_PER_STEP)]
        + [_page_spec(layer, s, +1) for s in range(PAGES_PER_STEP)],
        out_specs=heads,
        scratch_shapes=[pltpu.VMEM((N_HEADS, LANES), F32), pltpu.VMEM((N_HEADS, LANES), F32),
                        pltpu.VMEM((N_HEADS, LANES), F32),
                        pltpu.VMEM((nb, N_HEADS, HEAD_DIM, PAGE_SIZE), F32)],
    )
    return pl.pallas_call(
        _moba_decode_kernel,
        grid_spec=grid_spec,
        out_shape=jax.ShapeDtypeStruct((batch, N_HEADS, HEAD_DIM), F32),
        compiler_params=_params("parallel", "arbitrary"),
        name="moba_decode",
    )(page_table, q[..., None], q, k_new, v_new, dec, far, own0,
      *([cache_k] * PAGES_PER_STEP), *([cache_v] * PAGES_PER_STEP))


def _sb_decode_kernel(pt_ref, q_ref, *refs):
    del pt_ref
    n_pg = PAGES_PER_STEP
    k_pages, v_pages = refs[:n_pg], refs[n_pg:2 * n_pg]
    o_ref, carry_ref, acc_ref = refs[2 * n_pg:]
    c = pl.program_id(1)
    qb = jnp.broadcast_to(q_ref[...] * SCALE, (N_HEADS, HEAD_DIM, PAGE_SIZE))
    lane = lax.broadcasted_iota(jnp.int32, (1, PAGE_SIZE), 1)

    @pl.when(c == 0)
    def _():
        carry_ref[...] = jnp.zeros_like(carry_ref)
        acc_ref[...] = jnp.zeros_like(acc_ref)

    carry = carry_ref[...]
    acc = None
    for j in range(n_pg):
        z = _page_scores(qb, k_pages[j])
        sp = _softplus_abs(z)
        log_keep = -sp
        incl = log_keep
        shift = 1
        while shift < PAGE_SIZE:
            moved = pltpu.roll(incl, PAGE_SIZE - shift, 1)
            incl = incl + jnp.where(lane < PAGE_SIZE - shift, moved, 0.0)
            shift *= 2
        w = jnp.exp(z - sp + incl - log_keep + carry)
        wv = _weighted_values(w, v_pages[j])
        acc = wv if acc is None else acc + wv
        carry = carry + jnp.sum(log_keep, axis=-1, keepdims=True)
    carry_ref[...] = carry
    acc_ref[...] += acc

    @pl.when(c == pl.num_programs(1) - 1)
    def _():
        o_ref[...] = _sum_positions(acc_ref[...])


def _sb_decode(page_table, q, cache_k, cache_v, *, layer):
    batch, n_pages = page_table.shape
    n_steps = n_pages // PAGES_PER_STEP
    grid_spec = pltpu.PrefetchScalarGridSpec(
        num_scalar_prefetch=1,
        grid=(batch, n_steps),
        in_specs=[pl.BlockSpec((None, N_HEADS, HEAD_DIM, 1), lambda b, c, pt: (b, 0, 0, 0))]
        + [_page_spec(layer, s, -1) for s in range(PAGES_PER_STEP)]
        + [_page_spec(layer, s, -1) for s in range(PAGES_PER_STEP)],
        out_specs=pl.BlockSpec((None, N_HEADS, HEAD_DIM), lambda b, c, pt: (b, 0, 0)),
        scratch_shapes=[pltpu.VMEM((N_HEADS, 1), F32),
                        pltpu.VMEM((N_HEADS, HEAD_DIM, PAGE_SIZE), F32)],
    )
    return pl.pallas_call(
        _sb_decode_kernel,
        grid_spec=grid_spec,
        out_shape=jax.ShapeDtypeStruct((batch, N_HEADS, HEAD_DIM), F32),
        compiler_params=_params("parallel", "arbitrary"),
        name="sb_decode",
    )(page_table, q[..., None], *([cache_k] * PAGES_PER_STEP), *([cache_v] * PAGES_PER_STEP))


def _block_diag(w):
    nblk, bk, bj = w.shape
    eye = jnp.eye(nblk, dtype=w.dtype)
    return jnp.einsum("nkj,nm->nkmj", w, eye).reshape(nblk * bk, nblk * bj)


def _layer_weights(l, norm_ffn1, ffn1_w1, ffn1_w3, ffn1_w2, norm_mix, w_in, b_gate, conv_w, conv_b,
                   lru_wa, lru_ba, lru_wx, lru_bx, lru_lambda, w_branch, w_out, norm_ffn2,
                   ffn2_w1, ffn2_w3, ffn2_w2):
    vec = lambda a: a[l].reshape(1, -1)
    bf = lambda a: a.astype(BF16)
    return {
        "ln1": vec(norm_ffn1), "f1_w1": bf(ffn1_w1), "f1_w3": bf(ffn1_w3), "f1_w2": bf(ffn1_w2),
        "ln_mix": vec(norm_mix), "w_in": bf(w_in), "b_gate": b_gate[l],
        "lru": {"conv_w": conv_w[l], "conv_b": vec(conv_b),
                "wa": _block_diag(lru_wa[l]).astype(BF16), "ba": vec(lru_ba),
                "wx": _block_diag(lru_wx[l]).astype(BF16), "bx": vec(lru_bx), "lam": vec(lru_lambda)},
        "w_branch": bf(w_branch), "w_out": bf(w_out),
        "ln2": vec(norm_ffn2), "f2_w1": bf(ffn2_w1), "f2_w3": bf(ffn2_w3), "f2_w2": bf(ffn2_w2),
    }


FFN_TILE_F = 256


def kernel(x_prompt, x_sample, cache_moba_k, cache_moba_v, cache_sb_k, cache_sb_v, state_lru_h,
           state_lru_conv, page_table, rel_bias, norm_ffn1, ffn1_w1, ffn1_w3, ffn1_w2, norm_mix, w_in,
           b_gate, conv_w, conv_b, lru_wa, lru_ba, lru_wx, lru_bx, lru_lambda, w_branch, w_out,
           norm_ffn2, ffn2_w1, ffn2_w3, ffn2_w2, norm_final):
    bp, seq, d = x_prompt.shape
    db = x_sample.shape[0]
    depth = w_in.shape[0]
    n_phys = cache_moba_k.shape[1]
    toeplitz, far, dec, far_dec, own0 = _bias_tables(rel_bias)
    gf = norm_final.reshape(1, d)
    paged = lambda c: jnp.transpose(c, (0, 1, 3, 4, 2))
    cmk, cmv, csk, csv = paged(cache_moba_k), paged(cache_moba_v), paged(cache_sb_k), paged(cache_sb_v)

    yp = x_prompt.reshape(bp * seq, d)
    ys = x_sample.reshape(db, d)
    tm_p = 1024 if (bp * seq) % 1024 == 0 else MOBA_BLOCK
    zeros_buf = jnp.zeros((bp, CONV_W - 1, LRU_W), F32)
    zeros_h = jnp.zeros((bp, 1, LRU_W), F32)
    outs = {k: [] for k in ("p_mk", "p_mv", "p_sk", "p_sv", "p_h", "p_cv",
                            "s_mk", "s_mv", "s_sk", "s_sv", "s_h", "s_cv")}
    for l in range(depth):
        lw = _layer_weights(l, norm_ffn1, ffn1_w1, ffn1_w3, ffn1_w2, norm_mix, w_in, b_gate, conv_w,
                            conv_b, lru_wa, lru_ba, lru_wx, lru_bx, lru_lambda, w_branch, w_out,
                            norm_ffn2, ffn2_w1, ffn2_w3, ffn2_w2)
        last = l == depth - 1

        yp = _ffn(yp, lw["ln1"], lw["f1_w1"], lw["f1_w3"], lw["f1_w2"], gf,
                  layer=l, final_norm=False, tm=tm_p, tf=FFN_TILE_F)
        qa, ka, va, xb, gb, qc, kc, vc, ka_bf, ka_means, vat, kc_bf, _, vct = _inproj(
            yp, lw["ln_mix"], lw["w_in"], layer=l, tm=max(tm_p // 2, MOBA_BLOCK), seq=seq)
        o_a = _moba_prompt(qa, ka_means.reshape(bp * seq // MOBA_BLOCK, ATT_W), ka_bf, vat, toeplitz, far,
                           batch=bp, seq=seq)
        o_b, h_new, conv_new = _lru_prompt(xb, gb, zeros_buf, zeros_h, lw["lru"],
                                           batch=bp, seq=seq, tc=min(seq, 512))
        o_c = _sb_prompt(qc, kc_bf, vct, batch=bp, seq=seq)
        yp = _merge(o_a, o_b, o_c, yp, lw["ln_mix"], lw["b_gate"], lw["w_in"], lw["w_branch"], lw["w_out"],
                    layer=l, tm=MOBA_BLOCK, transposed_attention=True)
        yp = _ffn(yp, lw["ln2"], lw["f2_w1"], lw["f2_w3"], lw["f2_w2"], gf,
                  layer=l, final_norm=last, tm=tm_p, tf=FFN_TILE_F)
        outs["p_mk"].append(ka); outs["p_mv"].append(va)
        outs["p_sk"].append(kc); outs["p_sv"].append(vc)
        outs["p_h"].append(h_new.reshape(bp, LRU_W)); outs["p_cv"].append(conv_new)

        ys = _ffn(ys, lw["ln1"], lw["f1_w1"], lw["f1_w3"], lw["f1_w2"], gf,
                  layer=l, final_norm=False, tm=db, tf=FFN_TILE_F)
        qa, ka, va, xb, gb, qc, kc, vc = _inproj(ys, lw["ln_mix"], lw["w_in"], layer=l, tm=db)
        by_head = lambda a: a.reshape(db, N_HEADS, HEAD_DIM)
        o_a = _moba_decode(page_table, by_head(qa), by_head(ka), by_head(va), cmk, cmv,
                           dec, far_dec, own0, layer=l).reshape(db, ATT_W)
        o_b, h_new, conv_new = _lru_step(xb, gb, jnp.transpose(state_lru_conv[l], (1, 0, 2)),
                                         state_lru_h[l], lw["lru"])
        o_c = _sb_decode(page_table, by_head(qc), csk, csv, layer=l).reshape(db, ATT_W)
        ys = _merge(o_a, o_b, o_c, ys, lw["ln_mix"], lw["b_gate"], lw["w_in"], lw["w_branch"], lw["w_out"],
                    layer=l, tm=db, transposed_attention=False)
        ys = _ffn(ys, lw["ln2"], lw["f2_w1"], lw["f2_w3"], lw["f2_w2"], gf,
                  layer=l, final_norm=last, tm=db, tf=FFN_TILE_F)
        kv_shape = (db, 1, N_HEADS, HEAD_DIM)
        outs["s_mk"].append(ka.reshape(kv_shape)); outs["s_mv"].append(va.reshape(kv_shape))
        outs["s_sk"].append(kc.reshape(kv_shape)); outs["s_sv"].append(vc.reshape(kv_shape))
        outs["s_h"].append(h_new); outs["s_cv"].append(jnp.transpose(conv_new, (1, 0, 2)))

    st = {k: jnp.stack(v) for k, v in outs.items()}
    for k in ("p_mk", "p_mv", "p_sk", "p_sv"):
        st[k] = jnp.transpose(st[k].reshape(depth, bp, N_HEADS, HEAD_DIM, seq), (0, 1, 4, 2, 3))
    return (yp.reshape(bp, seq, d), ys.reshape(db, 1, d),
            st["p_mk"], st["p_mv"], st["p_sk"], st["p_sv"], st["p_h"], st["p_cv"],
            st["s_mk"], st["s_mv"], st["s_sk"], st["s_sv"], st["s_h"], st["s_cv"])
```

```python
import functools
import math

import jax
import jax.numpy as jnp
from jax import lax
from jax.experimental import pallas as pl
from jax.experimental.pallas import tpu as pltpu

F32 = jnp.float32
BF16 = jnp.bfloat16

HEAD_DIM = 64
N_HEADS = 8
ATT_W = N_HEADS * HEAD_DIM
LANES = 128
HEADS_PER_TILE = LANES // HEAD_DIM
N_PAIRS = ATT_W // LANES
MOBA_BLOCK = 256
MOBA_TOPK = 3
PAGE_SIZE = 128
PAGES_PER_MOBA_BLOCK = MOBA_BLOCK // PAGE_SIZE
PAGES_PER_STEP = 16
LRU_W = 512
LRU_BLOCKS = 8
CONV_W = 4
RG_C = 8.0
N_BUCKETS = 32
MAX_DIST = 128
RMS_EPS = 1e-6
NEG = -1e30
SCALE = HEAD_DIM ** -0.5
VMEM_LIMIT = 56 * 1024 * 1024

_NT = (((1,), (1,)), ((), ()))


def _params(*sem):
    return pltpu.CompilerParams(dimension_semantics=sem, vmem_limit_bytes=VMEM_LIMIT)


def _rms(x, g):
    return x * lax.rsqrt(jnp.mean(x * x, axis=-1, keepdims=True) + RMS_EPS) * g


def _softplus(z):
    return jnp.maximum(z, 0.0) + jnp.log1p(jnp.exp(-jnp.abs(z)))


def _softplus_abs(z):
    return jnp.maximum(z, 0.0) + jnp.log(1.0 + jnp.exp(-jnp.abs(z)))


def _expm1(x):
    u = jnp.exp(x)
    um1 = u - 1.0
    return jnp.where(um1 == 0.0, x, jnp.where(um1 == -1.0, -1.0, um1 * x / jnp.log(u)))


def _dot_nt(a, b):
    return lax.dot_general(a.astype(BF16), b.astype(BF16), _NT, preferred_element_type=F32)


def _ffn_kernel(x_ref, g_ref, w1_ref, w3_ref, w2_ref, gf_ref, o_ref, h_ref, acc_ref, *, final_norm):
    j = pl.program_id(1)

    @pl.when(j == 0)
    def _():
        h_ref[...] = _rms(x_ref[...], g_ref[...]).astype(BF16)
        acc_ref[...] = jnp.zeros_like(acc_ref)

    h = h_ref[...]
    a = jnp.dot(h, w1_ref[...], preferred_element_type=F32)
    b = jnp.dot(h, w3_ref[...], preferred_element_type=F32)
    u = a * jax.nn.sigmoid(a) * b
    acc_ref[...] += jnp.dot(u.astype(BF16), w2_ref[...], preferred_element_type=F32)

    @pl.when(j == pl.num_programs(1) - 1)
    def _():
        y = x_ref[...] + 0.5 * acc_ref[...]
        if final_norm:
            y = _rms(y, gf_ref[...])
        o_ref[...] = y


def _ffn(x, g, w1, w3, w2, gf, *, layer, final_norm, tm, tf):
    n, d = x.shape
    dff = w1.shape[2]
    return pl.pallas_call(
        functools.partial(_ffn_kernel, final_norm=final_norm),
        grid=(n // tm, dff // tf),
        in_specs=[
            pl.BlockSpec((tm, d), lambda i, j: (i, 0)),
            pl.BlockSpec((1, d), lambda i, j: (0, 0)),
            pl.BlockSpec((None, d, tf), lambda i, j: (layer, 0, j)),
            pl.BlockSpec((None, d, tf), lambda i, j: (layer, 0, j)),
            pl.BlockSpec((None, tf, d), lambda i, j: (layer, j, 0)),
            pl.BlockSpec((1, d), lambda i, j: (0, 0)),
        ],
        out_specs=pl.BlockSpec((tm, d), lambda i, j: (i, 0)),
        out_shape=jax.ShapeDtypeStruct((n, d), F32),
        scratch_shapes=[pltpu.VMEM((tm, d), BF16), pltpu.VMEM((tm, d), F32)],
        compiler_params=_params("parallel", "arbitrary"),
        name="ffn",
    )(x, g, w1, w3, w2, gf)


N_IN_SPLIT = 8
K_SLOTS = (1, 6)
V_SLOTS = (2, 7)


def _inproj_kernel(x_ref, g_ref, w_ref, *refs, prompt_layouts):
    outs, extra, h_ref = refs[:N_IN_SPLIT], refs[N_IN_SPLIT:-1], refs[-1]
    j = pl.program_id(1)

    @pl.when(j == 0)
    def _():
        h_ref[...] = _rms(x_ref[...], g_ref[...]).astype(BF16)

    for k in range(N_IN_SPLIT // 2):
        @pl.when(j == k)
        def _(k=k):
            p = jnp.dot(h_ref[...], w_ref[...], preferred_element_type=F32)
            for slot in (2 * k, 2 * k + 1):
                part = p[:, (slot % 2) * ATT_W:(slot % 2 + 1) * ATT_W]
                if not prompt_layouts or slot not in K_SLOTS + V_SLOTS:
                    outs[slot][...] = part
                    continue
                part_t = part.T
                outs[slot][...] = part_t
                if slot in K_SLOTS:
                    kb_ref, kmean_ref = extra[3 * K_SLOTS.index(slot)], extra[3 * K_SLOTS.index(slot) + 1]
                    kb_ref[...] = part.astype(BF16)
                    for blk in range(kmean_ref.shape[0]):
                        kmean_ref[blk:blk + 1, :] = jnp.mean(
                            part[blk * MOBA_BLOCK:(blk + 1) * MOBA_BLOCK, :], axis=0, keepdims=True)
                else:
                    vt_ref = extra[3 * V_SLOTS.index(slot) + 2]
                    for blk in range(vt_ref.shape[0]):
                        vt_ref[blk] = part_t[:, blk * MOBA_BLOCK:(blk + 1) * MOBA_BLOCK].astype(BF16)


def _inproj(x, g, w_in, *, layer, tm, seq=None):
    n, d = x.shape
    tn = 2 * ATT_W
    out_specs = [pl.BlockSpec((tm, ATT_W), lambda i, j: (i, 0)) for _ in range(N_IN_SPLIT)]
    out_shape = [jax.ShapeDtypeStruct((n, ATT_W), F32) for _ in range(N_IN_SPLIT)]
    if seq is not None:
        tiles_per_seq = seq // tm
        blocks_per_tile = tm // MOBA_BLOCK
        for slot in K_SLOTS + V_SLOTS:
            out_specs[slot] = pl.BlockSpec((None, ATT_W, tm),
                                           lambda i, j: (i // tiles_per_seq, 0, i % tiles_per_seq))
            out_shape[slot] = jax.ShapeDtypeStruct((n // seq, ATT_W, seq), F32)
        for _ in range(2):
            out_specs.append(pl.BlockSpec((tm, ATT_W), lambda i, j: (i, 0)))
            out_shape.append(jax.ShapeDtypeStruct((n, ATT_W), BF16))
            out_specs.append(pl.BlockSpec((None, blocks_per_tile, ATT_W), lambda i, j: (i, 0, 0)))
            out_shape.append(jax.ShapeDtypeStruct((n // tm, blocks_per_tile, ATT_W), F32))
            out_specs.append(pl.BlockSpec((blocks_per_tile, ATT_W, MOBA_BLOCK), lambda i, j: (i, 0, 0)))
            out_shape.append(jax.ShapeDtypeStruct((n // MOBA_BLOCK, ATT_W, MOBA_BLOCK), BF16))
    return pl.pallas_call(
        functools.partial(_inproj_kernel, prompt_layouts=seq is not None),
        grid=(n // tm, N_IN_SPLIT * ATT_W // tn),
        in_specs=[
            pl.BlockSpec((tm, d), lambda i, j: (i, 0)),
            pl.BlockSpec((1, d), lambda i, j: (0, 0)),
            pl.BlockSpec((None, d, tn), lambda i, j: (layer, 0, j)),
        ],
        out_specs=out_specs,
        out_shape=out_shape,
        scratch_shapes=[pltpu.VMEM((tm, d), BF16)],
        compiler_params=_params("parallel", "arbitrary"),
        name="inproj",
    )(x, g, w_in)


N_BRANCH = 3


def _merge_kernel(oa_ref, ob_ref, oc_ref, x_ref, g_ref, bg_ref, *refs, transposed_attention):
    wg_refs, (wb_ref, wo_ref, o_ref) = refs[:N_BRANCH], refs[N_BRANCH:]
    x = x_ref[...]
    h = _rms(x, g_ref[...]).astype(BF16)
    merged = None
    for n, o_n in enumerate((oa_ref, ob_ref, oc_ref)):
        branch = o_n[...]
        if transposed_attention and n != 1:
            branch = branch.T
        proj = jnp.dot(branch.astype(BF16), wb_ref[n], preferred_element_type=F32)
        logits = jnp.dot(h, wg_refs[n][...], preferred_element_type=F32)
        gate = jax.nn.sigmoid(logits + bg_ref[n:n + 1, :])
        merged = gate * proj if merged is None else merged + gate * proj
    o_ref[...] = x + jnp.dot(merged.astype(BF16), wo_ref[...], preferred_element_type=F32)


def _merge(oa, ob, oc, x, g, bg, w_in, wb, wo, *, layer, tm, transposed_attention):
    n, d = x.shape
    first_gate_tile = w_in.shape[2] // d - N_BRANCH
    row = lambda w: pl.BlockSpec((tm, w), lambda i: (i, 0))
    att = pl.BlockSpec((None, ATT_W, tm), lambda i: (i, 0, 0)) if transposed_attention else row(ATT_W)
    gate_w = [pl.BlockSpec((None, d, d), lambda i, t=first_gate_tile + k: (layer, 0, t))
              for k in range(N_BRANCH)]
    return pl.pallas_call(
        functools.partial(_merge_kernel, transposed_attention=transposed_attention),
        grid=(n // tm,),
        in_specs=[att, row(ATT_W), att, row(d),
                  pl.BlockSpec((1, d), lambda i: (0, 0)),
                  pl.BlockSpec(bg.shape, lambda i: (0, 0)),
                  *gate_w,
                  pl.BlockSpec((None,) + wb.shape[1:], lambda i: (layer, 0, 0, 0)),
                  pl.BlockSpec((None,) + wo.shape[1:], lambda i: (layer, 0, 0))],
        out_specs=row(d),
        out_shape=jax.ShapeDtypeStruct((n, d), F32),
        compiler_params=_params("parallel"),
        name="merge",
    )(oa, ob, oc, x, g, bg, *([w_in] * N_BRANCH), wb, wo)


def _rel_bucket(dist):
    max_exact = N_BUCKETS // 2
    d = jnp.maximum(dist, 0)
    df = jnp.maximum(d, 1).astype(F32)
    large = max_exact + (jnp.log(df / max_exact) / math.log(MAX_DIST / max_exact)
                         * (N_BUCKETS - max_exact)).astype(jnp.int32)
    large = jnp.minimum(large, N_BUCKETS - 1)
    return jnp.where(d < max_exact, d, large)


def _bias_tables(rel_bias):
    period = 4 * MOBA_BLOCK
    k = jnp.arange(period, dtype=jnp.int32)
    k = jnp.where(k < period // 2, k, k - period)
    u = rel_bias[_rel_bucket(jnp.maximum(MOBA_BLOCK + k, 0))].T
    n_keys = 2 * MOBA_BLOCK
    toeplitz = jnp.tile(u, (1, n_keys))[:, :n_keys * (period - 1)]
    toeplitz = toeplitz.reshape(N_HEADS, n_keys, period - 1)[:, :, :MOBA_BLOCK]
    heads_per_step = STEP_W // HEAD_DIM
    n_steps = N_HEADS // heads_per_step
    toeplitz = toeplitz.reshape(n_steps, heads_per_step, n_keys, MOBA_BLOCK)
    toeplitz = jnp.transpose(toeplitz, (0, 2, 1, 3)).reshape(n_steps, n_keys, STEP_Q)
    far = rel_bias[_rel_bucket(jnp.int32(2 * MOBA_BLOCK))]
    far_pair = jnp.repeat(far.reshape(n_steps, 1, heads_per_step), MOBA_BLOCK, axis=2)
    dec = rel_bias[_rel_bucket(MOBA_BLOCK - jnp.arange(MOBA_BLOCK, dtype=jnp.int32))].T
    own0 = rel_bias[_rel_bucket(jnp.int32(0))][:, None]
    return toeplitz, far_pair, dec, far[:, None], own0


PAIR_Q = HEADS_PER_TILE * MOBA_BLOCK
TILES_PER_STEP = 4
STEP_W = TILES_PER_STEP * LANES
STEP_Q = TILES_PER_STEP * PAIR_Q


def _stacked_queries(q):
    lane = lax.broadcasted_iota(jnp.int32, (1, LANES), 1)
    stacks = []
    for g in range(TILES_PER_STEP):
        qs = q[:, g * LANES:(g + 1) * LANES] * SCALE
        stacks.append(jnp.concatenate([jnp.where((lane // HEAD_DIM) == hh, qs, 0.0).astype(BF16)
                                       for hh in range(HEADS_PER_TILE)], axis=0))
    return stacks


def _scores(k_blk, qcat):
    return jnp.concatenate([_dot_nt(k_blk[:, g * LANES:(g + 1) * LANES], qcat[g])
                            for g in range(TILES_PER_STEP)], axis=1)


def _values(vt_blk, w):
    return jnp.concatenate([jnp.dot(vt_blk[g * LANES:(g + 1) * LANES, :], w[:, g * PAIR_Q:(g + 1) * PAIR_Q],
                                    preferred_element_type=F32)
                            for g in range(TILES_PER_STEP)], axis=0)


def _per_head_rows(stat):
    return jnp.concatenate([jnp.broadcast_to(stat[:, g * PAIR_Q:(g + 1) * PAIR_Q], (LANES, PAIR_Q))
                            for g in range(TILES_PER_STEP)], axis=0)


def _pair_corners(acc):
    rows = []
    for r in range(STEP_W // HEAD_DIM):
        hh = r % HEADS_PER_TILE
        rows.append(acc[r * HEAD_DIM:(r + 1) * HEAD_DIM, hh * MOBA_BLOCK:(hh + 1) * MOBA_BLOCK])
    return jnp.concatenate(rows, axis=0)


def _moba_prompt_kernel(q_ref, kmean_ref, kb_ref, vt_ref, tb_ref, far_ref, o_ref, sel_ref):
    i = pl.program_id(2)
    nb = kb_ref.shape[0] // MOBA_BLOCK
    q = q_ref[...]
    qcat = _stacked_queries(q)
    lane = lax.broadcasted_iota(jnp.int32, (1, LANES), 1)
    blk_row = lax.broadcasted_iota(jnp.int32, (nb, 1), 0)

    gate = jnp.concatenate(
        [lax.dot_general(jnp.where((lane // HEAD_DIM) == hh, kmean_ref[:, g * LANES:(g + 1) * LANES], 0.0),
                         q[:, g * LANES:(g + 1) * LANES], _NT,
                         precision=lax.Precision.HIGHEST, preferred_element_type=F32)
         for g in range(TILES_PER_STEP) for hh in range(HEADS_PER_TILE)], axis=1)
    gate = jnp.where(blk_row < i, gate, NEG)
    rank = jnp.zeros_like(gate)
    for m in range(nb):
        gm = gate[m:m + 1, :]
        ahead = (gm > gate) | ((gm == gate) & (blk_row > m))
        rank = rank + ahead.astype(F32)
    sel_ref[...] = ((rank < MOBA_TOPK) & (blk_row < i)).astype(F32)

    def block(n, bias, keep, m_run, l_run, acc):
        start = pl.multiple_of(n * MOBA_BLOCK, MOBA_BLOCK)
        s = _scores(kb_ref[pl.ds(start, MOBA_BLOCK), :], qcat) + bias
        s = jnp.where(keep, s, NEG)
        m_new = jnp.max(s, axis=0, keepdims=True)
        if m_run is not None:
            m_new = jnp.maximum(m_run, m_new)
        p = jnp.exp(s - m_new)
        l_new = jnp.sum(p, axis=0, keepdims=True)
        pv = _values(vt_ref[n], p.astype(BF16))
        if m_run is None:
            return m_new, l_new, pv
        alpha = jnp.exp(m_run - m_new)
        return m_new, alpha * l_run + l_new, _per_head_rows(alpha) * acc + pv

    key_i = lax.broadcasted_iota(jnp.int32, (MOBA_BLOCK, STEP_Q), 0)
    qry_i = lax.broadcasted_iota(jnp.int32, (MOBA_BLOCK, STEP_Q), 1) % MOBA_BLOCK
    state0 = block(i, tb_ref[MOBA_BLOCK:, :], key_i <= qry_i, None, None, None)

    def body(n, state):
        bias = jnp.where(n == i - 1, tb_ref[:MOBA_BLOCK, :], far_ref[...])
        return block(n, bias, sel_ref[pl.ds(n, 1), :] > 0.5, *state)

    _, l_fin, acc_fin = lax.fori_loop(0, i, body, state0)
    o_ref[...] = _pair_corners(acc_fin / _per_head_rows(l_fin))


def _moba_prompt(q, k_means, k_bf, vt, toeplitz, far, *, batch, seq):
    nb = seq // MOBA_BLOCK
    return pl.pallas_call(
        _moba_prompt_kernel,
        grid=(batch, ATT_W // STEP_W, nb),
        in_specs=[
            pl.BlockSpec((MOBA_BLOCK, STEP_W), lambda b, p, i: (b * nb + i, p)),
            pl.BlockSpec((nb, STEP_W), lambda b, p, i: (b, p)),
            pl.BlockSpec((seq, STEP_W), lambda b, p, i: (b, p)),
            pl.BlockSpec((nb, STEP_W, MOBA_BLOCK), lambda b, p, i: (b, p, 0)),
            pl.BlockSpec((None, 2 * MOBA_BLOCK, STEP_Q), lambda b, p, i: (p, 0, 0)),
            pl.BlockSpec((None, 1, STEP_Q), lambda b, p, i: (p, 0, 0)),
        ],
        out_specs=pl.BlockSpec((None, STEP_W, MOBA_BLOCK), lambda b, p, i: (b * nb + i, p, 0)),
        out_shape=jax.ShapeDtypeStruct((batch * nb, ATT_W, MOBA_BLOCK), F32),
        scratch_shapes=[pltpu.VMEM((nb, STEP_Q), F32)],
        compiler_params=_params("parallel", "parallel", "arbitrary"),
        name="moba_prompt",
    )(q, k_means, k_bf, vt, toeplitz, far)


SB_BLOCK = 256
F32_EXP_UNDERFLOW = -105.0


def _sb_prompt_kernel(q_ref, kb_ref, vt_ref, o_ref):
    i = pl.program_id(2)
    qcat = _stacked_queries(q_ref[...])
    row_i = lax.broadcasted_iota(jnp.int32, (SB_BLOCK, SB_BLOCK), 0)
    col_i = lax.broadcasted_iota(jnp.int32, (SB_BLOCK, SB_BLOCK), 1)
    later = (row_i < col_i).astype(BF16)

    def block(n, mask, carry):
        start = pl.multiple_of(n * SB_BLOCK, SB_BLOCK)
        z = _scores(kb_ref[pl.ds(start, SB_BLOCK), :], qcat)
        sp = _softplus_abs(z)
        log_keep = -sp if mask is None else jnp.where(mask, -sp, 0.0)
        hi = log_keep.astype(BF16)
        lo = (log_keep - hi.astype(F32)).astype(BF16)
        both = jnp.dot(later, jnp.concatenate([hi, lo], axis=1), preferred_element_type=F32)
        between = both[:, :STEP_Q] + both[:, STEP_Q:]
        if carry is not None:
            between = between + carry
        w = jnp.exp(z - sp + between)
        if mask is not None:
            w = jnp.where(mask, w, 0.0)
        pv = _values(vt_ref[n], w.astype(BF16))
        return jnp.sum(log_keep, axis=0, keepdims=True), pv

    key_i = lax.broadcasted_iota(jnp.int32, (SB_BLOCK, STEP_Q), 0)
    qry_i = lax.broadcasted_iota(jnp.int32, (SB_BLOCK, STEP_Q), 1) % SB_BLOCK
    state0 = block(i, key_i < qry_i, None)

    def more(state):
        t, carry, _ = state
        return jnp.logical_and(t < i, jnp.max(carry) > F32_EXP_UNDERFLOW)

    def body(state):
        t, carry, acc = state
        kept, pv = block(i - 1 - t, None, carry)
        return t + 1, carry + kept, acc + pv

    _, _, acc_fin = lax.while_loop(more, body, (jnp.int32(0),) + state0)
    o_ref[...] = _pair_corners(acc_fin)


def _sb_prompt(q, k_bf, vt, *, batch, seq):
    nb = seq // SB_BLOCK
    return pl.pallas_call(
        _sb_prompt_kernel,
        grid=(batch, ATT_W // STEP_W, nb),
        in_specs=[
            pl.BlockSpec((SB_BLOCK, STEP_W), lambda b, p, i: (b * nb + i, p)),
            pl.BlockSpec((seq, STEP_W), lambda b, p, i: (b, p)),
            pl.BlockSpec((nb, STEP_W, SB_BLOCK), lambda b, p, i: (b, p, 0)),
        ],
        out_specs=pl.BlockSpec((None, STEP_W, SB_BLOCK), lambda b, p, i: (b * nb + i, p, 0)),
        out_shape=jax.ShapeDtypeStruct((batch * nb, ATT_W, SB_BLOCK), F32),
        compiler_params=_params("parallel", "parallel", "arbitrary"),
        name="sb_prompt",
    )(q, k_bf, vt)


def _lru_gates(xc, wa_ref, ba_ref, wx_ref, bx_ref, lam_ref):
    xc_bf = xc.astype(BF16)
    r = jax.nn.sigmoid(jnp.dot(xc_bf, wa_ref[...], preferred_element_type=F32) + ba_ref[...])
    gi = jax.nn.sigmoid(jnp.dot(xc_bf, wx_ref[...], preferred_element_type=F32) + bx_ref[...])
    log_a = -RG_C * r * _softplus(-lam_ref[...])
    a = jnp.exp(log_a)
    b = jnp.sqrt(-_expm1(2.0 * log_a)) * (gi * xc)
    return a, b


def _lru_prompt_kernel(x_ref, g_ref, buf_ref, h0_ref, cw_ref, cb_ref, wa_ref, ba_ref, wx_ref, bx_ref,
                       lam_ref, y_ref, hlast_ref, conv_ref, tail_ref, hc_ref):
    t = pl.program_id(1)
    tc = x_ref.shape[0]
    tail_rows = tail_ref.shape[0]

    @pl.when(t == 0)
    def _():
        tail_ref[...] = jnp.zeros_like(tail_ref)
        tail_ref[tail_rows - (CONV_W - 1):, :] = buf_ref[...]
        hc_ref[...] = h0_ref[...]

    x = x_ref[...]
    xe = jnp.concatenate([tail_ref[...], x], axis=0)
    xc = cb_ref[...] + cw_ref[CONV_W - 1:CONV_W, :] * x
    for k in range(1, CONV_W):
        xc = xc + cw_ref[CONV_W - 1 - k:CONV_W - k, :] * pltpu.roll(xe, k, 0)[tail_rows:, :]
    a, b = _lru_gates(xc, wa_ref, ba_ref, wx_ref, bx_ref, lam_ref)

    row = lax.broadcasted_iota(jnp.int32, (tc, 1), 0)
    k = 1
    while k < tc:
        a_sh = jnp.where(row >= k, pltpu.roll(a, k, 0), 1.0)
        b_sh = jnp.where(row >= k, pltpu.roll(b, k, 0), 0.0)
        b = a * b_sh + b
        a = a * a_sh
        k *= 2
    h = a * hc_ref[...] + b
    y_ref[...] = h * jax.nn.gelu(g_ref[...], approximate=True)
    hc_ref[...] = h[tc - 1:tc, :]
    tail_ref[...] = x[tc - tail_rows:, :]

    @pl.when(t == pl.num_programs(1) - 1)
    def _():
        hlast_ref[...] = h[tc - 1:tc, :]
        conv_ref[...] = x[tc - (CONV_W - 1):, :]


def _lru_prompt(x, g, buf, h0, lw, *, batch, seq, tc):
    nt = seq // tc
    w = x.shape[1]
    vec = pl.BlockSpec((1, w), lambda b, t: (0, 0))
    mat = pl.BlockSpec((w, w), lambda b, t: (0, 0))
    return pl.pallas_call(
        _lru_prompt_kernel,
        grid=(batch, nt),
        in_specs=[
            pl.BlockSpec((tc, w), lambda b, t: (b * nt + t, 0)),
            pl.BlockSpec((tc, w), lambda b, t: (b * nt + t, 0)),
            pl.BlockSpec((None, CONV_W - 1, w), lambda b, t: (b, 0, 0)),
            pl.BlockSpec((None, 1, w), lambda b, t: (b, 0, 0)),
            pl.BlockSpec((CONV_W, w), lambda b, t: (0, 0)),
            vec, mat, vec, mat, vec, vec,
        ],
        out_specs=[
            pl.BlockSpec((tc, w), lambda b, t: (b * nt + t, 0)),
            pl.BlockSpec((None, 1, w), lambda b, t: (b, 0, 0)),
            pl.BlockSpec((None, CONV_W - 1, w), lambda b, t: (b, 0, 0)),
        ],
        out_shape=[jax.ShapeDtypeStruct(x.shape, F32),
                   jax.ShapeDtypeStruct((batch, 1, w), F32),
                   jax.ShapeDtypeStruct((batch, CONV_W - 1, w), F32)],
        scratch_shapes=[pltpu.VMEM((8, w), F32), pltpu.VMEM((1, w), F32)],
        compiler_params=_params("parallel", "arbitrary"),
        name="lru_prompt",
    )(x, g, buf, h0, lw["conv_w"], lw["conv_b"], lw["wa"], lw["ba"], lw["wx"], lw["bx"], lw["lam"])


def _lru_step_kernel(x_ref, g_ref, buf_ref, h0_ref, cw_ref, cb_ref, wa_ref, ba_ref, wx_ref, bx_ref,
                     lam_ref, y_ref, h_ref, conv_ref):
    x = x_ref[...]
    xc = cb_ref[...] + cw_ref[CONV_W - 1:CONV_W, :] * x
    for k in range(CONV_W - 1):
        xc = xc + cw_ref[k:k + 1, :] * buf_ref[k]
    a, b = _lru_gates(xc, wa_ref, ba_ref, wx_ref, bx_ref, lam_ref)
    h = a * h0_ref[...] + b
    y_ref[...] = h * jax.nn.gelu(g_ref[...], approximate=True)
    h_ref[...] = h
    for k in range(CONV_W - 2):
        conv_ref[k] = buf_ref[k + 1]
    conv_ref[CONV_W - 2] = x


def _lru_step(x, g, buf, h0, lw):
    n, w = x.shape
    return pl.pallas_call(
        _lru_step_kernel,
        out_shape=[jax.ShapeDtypeStruct((n, w), F32), jax.ShapeDtypeStruct((n, w), F32),
                   jax.ShapeDtypeStruct((CONV_W - 1, n, w), F32)],
        compiler_params=pltpu.CompilerParams(vmem_limit_bytes=VMEM_LIMIT),
        name="lru_step",
    )(x, g, buf, h0, lw["conv_w"], lw["conv_b"], lw["wa"], lw["ba"], lw["wx"], lw["bx"], lw["lam"])


def _page_scores(qb, kt_ref):
    return jnp.concatenate([jnp.sum(qb[h] * kt_ref[h], axis=0, keepdims=True)
                            for h in range(N_HEADS)], axis=0)


def _weighted_values(w, vt_ref):
    return jnp.stack([jnp.broadcast_to(w[h:h + 1, :], (HEAD_DIM, PAGE_SIZE)) * vt_ref[h]
                      for h in range(N_HEADS)], axis=0)


def _sum_positions(acc):
    ones = jnp.ones((N_HEADS, PAGE_SIZE), F32)
    rows = [lax.dot_general(ones, acc[h], _NT, precision=lax.Precision.HIGHEST,
                            preferred_element_type=F32)[:1] for h in range(N_HEADS)]
    return jnp.concatenate(rows, axis=0)


def _moba_decode_kernel(pt_ref, q_ref, q2_ref, kn_ref, vn_ref, dec_ref, far_ref, own_ref, *refs):
    del pt_ref
    n_pg = PAGES_PER_STEP
    k_pages, v_pages = refs[:n_pg], refs[n_pg:2 * n_pg]
    o_ref, gate_ref, m_ref, l_ref, acc_ref = refs[2 * n_pg:]
    c = pl.program_id(1)
    n_steps = pl.num_programs(1)
    blocks_per_step = n_pg // PAGES_PER_MOBA_BLOCK
    qb = jnp.broadcast_to(q_ref[...] * SCALE, (N_HEADS, HEAD_DIM, PAGE_SIZE))
    lane = lax.broadcasted_iota(jnp.int32, (1, LANES), 1)

    @pl.when(c == 0)
    def _():
        gate_ref[...] = jnp.full_like(gate_ref, NEG)
        m_ref[...] = jnp.full_like(m_ref, NEG)
        l_ref[...] = jnp.zeros_like(l_ref)

    for j in range(blocks_per_step):
        n = c * blocks_per_step + j
        pages = range(PAGES_PER_MOBA_BLOCK * j, PAGES_PER_MOBA_BLOCK * (j + 1))
        z = [_page_scores(qb, k_pages[pg]) for pg in pages]
        g = sum(jnp.sum(zt, axis=-1, keepdims=True) for zt in z) * (1.0 / (SCALE * MOBA_BLOCK))
        newest = n == n_steps * blocks_per_step - 1
        s = [zt + jnp.where(newest, dec_ref[:, t * PAGE_SIZE:(t + 1) * PAGE_SIZE], far_ref[...])
             for t, zt in enumerate(z)]
        m = functools.reduce(jnp.maximum, [jnp.max(st, axis=-1, keepdims=True) for st in s])
        p = [jnp.exp(st - m) for st in s]
        l = sum(jnp.sum(pt, axis=-1, keepdims=True) for pt in p)
        acc_ref[n] = sum(_weighted_values(pt, v_pages[pg]) for pt, pg in zip(p, pages))
        gate_ref[...] = jnp.where(lane == n, g, gate_ref[...])
        m_ref[...] = jnp.where(lane == n, m, m_ref[...])
        l_ref[...] = jnp.where(lane == n, l, l_ref[...])

    @pl.when(c == n_steps - 1)
    def _():
        nb = n_steps * blocks_per_step
        gate = gate_ref[...]
        rank = jnp.zeros_like(gate)
        for mm in range(nb):
            gm = gate[:, mm:mm + 1]
            ahead = (gm > gate) | ((gm == gate) & (lane > mm))
            rank = rank + ahead.astype(F32)
        sel = (rank < MOBA_TOPK) & (lane < nb)
        s_own = jnp.sum(q2_ref[...] * SCALE * kn_ref[...], axis=-1, keepdims=True) + own_ref[...]
        m_sel = jnp.where(sel, m_ref[...], NEG)
        m_tot = jnp.maximum(jnp.max(m_sel, axis=-1, keepdims=True), s_own)
        wgt = jnp.where(sel, jnp.exp(m_sel - m_tot), 0.0)
        w_own = jnp.exp(s_own - m_tot)
        l_tot = jnp.sum(wgt * l_ref[...], axis=-1, keepdims=True) + w_own
        merged = []
        for h in range(N_HEADS):
            tot = jnp.zeros((HEAD_DIM, PAGE_SIZE), F32)
            for mm in range(nb):
                tot = tot + jnp.broadcast_to(wgt[h:h + 1, mm:mm + 1], (HEAD_DIM, PAGE_SIZE)) * acc_ref[mm, h]
            merged.append(tot)
        o = w_own * vn_ref[...] + _sum_positions(merged)
        o_ref[...] = o / l_tot


def _page_spec(layer, slot, order):
    def index_map(b, c, pt):
        n_pages = pt.shape[1]
        pos = c * PAGES_PER_STEP + slot
        return (layer, pt[b, pos if order > 0 else n_pages - 1 - pos], 0, 0, 0)
    return pl.BlockSpec((None, None, N_HEADS, HEAD_DIM, PAGE_SIZE), index_map)


def _moba_decode(page_table, q, k_new, v_new, cache_k, cache_v, dec, far, own0, *, layer):
    batch, n_pages = page_table.shape
    n_steps = n_pages // PAGES_PER_STEP
    nb = n_pages // PAGES_PER_MOBA_BLOCK
    heads = pl.BlockSpec((None, N_HEADS, HEAD_DIM), lambda b, c, pt: (b, 0, 0))
    column = pl.BlockSpec((None, N_HEADS, HEAD_DIM, 1), lambda b, c, pt: (b, 0, 0, 0))
    const = lambda a: pl.BlockSpec(a.shape, lambda b, c, pt: (0,) * a.ndim)
    grid_spec = pltpu.PrefetchScalarGridSpec(
        num_scalar_prefetch=1,
        grid=(batch, n_steps),
        in_specs=[column, heads, heads, heads, const(dec), const(far), const(own0)]
        + [_page_spec(layer, s, +1) for s in range(PAGES_PER_STEP)]
        + [_page_spec(layer, s, +1) for s in range(PAGES_PER_STEP)],
        out_specs=heads,
        scratch_shapes=[pltpu.VMEM((N_HEADS, LANES), F32), pltpu.VMEM((N_HEADS, LANES), F32),
                        pltpu.VMEM((N_HEADS, LANES), F32),
                        pltpu.VMEM((nb, N_HEADS, HEAD_DIM, PAGE_SIZE), F32)],
    )
    return pl.pallas_call(
        _moba_decode_kernel,
        grid_spec=grid_spec,
        out_shape=jax.ShapeDtypeStruct((batch, N_HEADS, HEAD_DIM), F32),
        compiler_params=_params("parallel", "arbitrary"),
        name="moba_decode",
    )(page_table, q[..., None], q, k_new, v_new, dec, far, own0,
      *([cache_k] * PAGES_PER_STEP), *([cache_v] * PAGES_PER_STEP))


SB_DECODE_SLOTS = 3


def _sb_decode_kernel(pt_ref, q_ref, k_hbm, v_hbm, o_ref, kbuf, vbuf, sem, *, layer):
    b = pl.program_id(0)
    n_pages = pt_ref.shape[1]
    ahead = SB_DECODE_SLOTS - 1
    qb = jnp.broadcast_to(q_ref[...] * SCALE, (N_HEADS, HEAD_DIM, PAGE_SIZE))
    lane = lax.broadcasted_iota(jnp.int32, (1, PAGE_SIZE), 1)

    def page_copies(j):
        page = pt_ref[b, n_pages - 1 - j]
        slot = j % SB_DECODE_SLOTS
        return (pltpu.make_async_copy(k_hbm.at[layer, page], kbuf.at[slot], sem.at[0, slot]),
                pltpu.make_async_copy(v_hbm.at[layer, page], vbuf.at[slot], sem.at[1, slot]))

    def start(j):
        for copy in page_copies(j):
            copy.start()

    def wait(j):
        for copy in page_copies(j):
            copy.wait()

    for j in range(ahead):
        start(j)

    def more(state):
        j, carry, _ = state
        return jnp.logical_and(j < n_pages, jnp.max(carry) > F32_EXP_UNDERFLOW)

    def body(state):
        j, carry, acc = state
        slot = j % SB_DECODE_SLOTS
        wait(j)

        @pl.when(j + ahead < n_pages)
        def _():
            start(j + ahead)

        z = _page_scores(qb, kbuf.at[slot])
        sp = _softplus_abs(z)
        log_keep = -sp
        incl = log_keep
        shift = 1
        while shift < PAGE_SIZE:
            moved = pltpu.roll(incl, PAGE_SIZE - shift, 1)
            incl = incl + jnp.where(lane < PAGE_SIZE - shift, moved, 0.0)
            shift *= 2
        w = jnp.exp(z - sp + incl - log_keep + carry)
        acc = acc + _weighted_values(w, vbuf.at[slot])
        return j + 1, carry + jnp.sum(log_keep, axis=-1, keepdims=True), acc

    state0 = (jnp.int32(0), jnp.zeros((N_HEADS, 1), F32), jnp.zeros((N_HEADS, HEAD_DIM, PAGE_SIZE), F32))
    j_end, _, acc = lax.while_loop(more, body, state0)

    for k in range(ahead):
        @pl.when(j_end + k < n_pages)
        def _(k=k):
            wait(j_end + k)

    o_ref[...] = _sum_positions(acc)


def _sb_decode(page_table, q, cache_k, cache_v, *, layer):
    batch, n_pages = page_table.shape
    assert n_pages >= SB_DECODE_SLOTS - 1
    page_buffer = pltpu.VMEM((SB_DECODE_SLOTS, N_HEADS, HEAD_DIM, PAGE_SIZE), F32)
    grid_spec = pltpu.PrefetchScalarGridSpec(
        num_scalar_prefetch=1,
        grid=(batch,),
        in_specs=[pl.BlockSpec((None, N_HEADS, HEAD_DIM, 1), lambda b, pt: (b, 0, 0, 0)),
                  pl.BlockSpec(memory_space=pl.ANY),
                  pl.BlockSpec(memory_space=pl.ANY)],
        out_specs=pl.BlockSpec((None, N_HEADS, HEAD_DIM), lambda b, pt: (b, 0, 0)),
        scratch_shapes=[page_buffer, page_buffer, pltpu.SemaphoreType.DMA((2, SB_DECODE_SLOTS))],
    )
    return pl.pallas_call(
        functools.partial(_sb_decode_kernel, layer=layer),
        grid_spec=grid_spec,
        out_shape=jax.ShapeDtypeStruct((batch, N_HEADS, HEAD_DIM), F32),
        compiler_params=_params("arbitrary"),
        name="sb_decode",
    )(page_table, q[..., None], cache_k, cache_v)


def _block_diag(w):
    nblk, bk, bj = w.shape
    eye = jnp.eye(nblk, dtype=w.dtype)
    return jnp.einsum("nkj,nm->nkmj", w, eye).reshape(nblk * bk, nblk * bj)


def _layer_weights(l, norm_ffn1, ffn1_w1, ffn1_w3, ffn1_w2, norm_mix, w_in, b_gate, conv_w, conv_b,
                   lru_wa, lru_ba, lru_wx, lru_bx, lru_lambda, w_branch, w_out, norm_ffn2,
                   ffn2_w1, ffn2_w3, ffn2_w2):
    vec = lambda a: a[l].reshape(1, -1)
    bf = lambda a: a.astype(BF16)
    return {
        "ln1": vec(norm_ffn1), "f1_w1": bf(ffn1_w1), "f1_w3": bf(ffn1_w3), "f1_w2": bf(ffn1_w2),
        "ln_mix": vec(norm_mix), "w_in": bf(w_in), "b_gate": b_gate[l],
        "lru": {"conv_w": conv_w[l], "conv_b": vec(conv_b),
                "wa": _block_diag(lru_wa[l]).astype(BF16), "ba": vec(lru_ba),
                "wx": _block_diag(lru_wx[l]).astype(BF16), "bx": vec(lru_bx), "lam": vec(lru_lambda)},
        "w_branch": bf(w_branch), "w_out": bf(w_out),
        "ln2": vec(norm_ffn2), "f2_w1": bf(ffn2_w1), "f2_w3": bf(ffn2_w3), "f2_w2": bf(ffn2_w2),
    }


FFN_TILE_F = 256


def kernel(x_prompt, x_sample, cache_moba_k, cache_moba_v, cache_sb_k, cache_sb_v, state_lru_h,
           state_lru_conv, page_table, rel_bias, norm_ffn1, ffn1_w1, ffn1_w3, ffn1_w2, norm_mix, w_in,
           b_gate, conv_w, conv_b, lru_wa, lru_ba, lru_wx, lru_bx, lru_lambda, w_branch, w_out,
           norm_ffn2, ffn2_w1, ffn2_w3, ffn2_w2, norm_final):
    bp, seq, d = x_prompt.shape
    db = x_sample.shape[0]
    depth = w_in.shape[0]
    n_phys = cache_moba_k.shape[1]
    toeplitz, far, dec, far_dec, own0 = _bias_tables(rel_bias)
    gf = norm_final.reshape(1, d)
    paged = lambda c: jnp.transpose(c, (0, 1, 3, 4, 2))
    cmk, cmv, csk, csv = paged(cache_moba_k), paged(cache_moba_v), paged(cache_sb_k), paged(cache_sb_v)

    yp = x_prompt.reshape(bp * seq, d)
    ys = x_sample.reshape(db, d)
    tm_p = 1024 if (bp * seq) % 1024 == 0 else MOBA_BLOCK
    zeros_buf = jnp.zeros((bp, CONV_W - 1, LRU_W), F32)
    zeros_h = jnp.zeros((bp, 1, LRU_W), F32)
    outs = {k: [] for k in ("p_mk", "p_mv", "p_sk", "p_sv", "p_h", "p_cv",
                            "s_mk", "s_mv", "s_sk", "s_sv", "s_h", "s_cv")}
    for l in range(depth):
        lw = _layer_weights(l, norm_ffn1, ffn1_w1, ffn1_w3, ffn1_w2, norm_mix, w_in, b_gate, conv_w,
                            conv_b, lru_wa, lru_ba, lru_wx, lru_bx, lru_lambda, w_branch, w_out,
                            norm_ffn2, ffn2_w1, ffn2_w3, ffn2_w2)
        last = l == depth - 1

        yp = _ffn(yp, lw["ln1"], lw["f1_w1"], lw["f1_w3"], lw["f1_w2"], gf,
                  layer=l, final_norm=False, tm=tm_p, tf=FFN_TILE_F)
        qa, ka, va, xb, gb, qc, kc, vc, ka_bf, ka_means, vat, kc_bf, _, vct = _inproj(
            yp, lw["ln_mix"], lw["w_in"], layer=l, tm=max(tm_p // 2, MOBA_BLOCK), seq=seq)
        o_a = _moba_prompt(qa, ka_means.reshape(bp * seq // MOBA_BLOCK, ATT_W), ka_bf, vat, toeplitz, far,
                           batch=bp, seq=seq)
        o_b, h_new, conv_new = _lru_prompt(xb, gb, zeros_buf, zeros_h, lw["lru"],
                                           batch=bp, seq=seq, tc=min(seq, 512))
        o_c = _sb_prompt(qc, kc_bf, vct, batch=bp, seq=seq)
        yp = _merge(o_a, o_b, o_c, yp, lw["ln_mix"], lw["b_gate"], lw["w_in"], lw["w_branch"], lw["w_out"],
                    layer=l, tm=MOBA_BLOCK, transposed_attention=True)
        yp = _ffn(yp, lw["ln2"], lw["f2_w1"], lw["f2_w3"], lw["f2_w2"], gf,
                  layer=l, final_norm=last, tm=tm_p, tf=FFN_TILE_F)
        outs["p_mk"].append(ka); outs["p_mv"].append(va)
        outs["p_sk"].append(kc); outs["p_sv"].append(vc)
        outs["p_h"].append(h_new.reshape(bp, LRU_W)); outs["p_cv"].append(conv_new)

        ys = _ffn(ys, lw["ln1"], lw["f1_w1"], lw["f1_w3"], lw["f1_w2"], gf,
                  layer=l, final_norm=False, tm=db, tf=FFN_TILE_F)
        qa, ka, va, xb, gb, qc, kc, vc = _inproj(ys, lw["ln_mix"], lw["w_in"], layer=l, tm=db)
        by_head = lambda a: a.reshape(db, N_HEADS, HEAD_DIM)
        o_a = _moba_decode(page_table, by_head(qa), by_head(ka), by_head(va), cmk, cmv,
                           dec, far_dec, own0, layer=l).reshape(db, ATT_W)
        o_b, h_new, conv_new = _lru_step(xb, gb, jnp.transpose(state_lru_conv[l], (1, 0, 2)),
                                         state_lru_h[l], lw["lru"])
        o_c = _sb_decode(page_table, by_head(qc), csk, csv, layer=l).reshape(db, ATT_W)
        ys = _merge(o_a, o_b, o_c, ys, lw["ln_mix"], lw["b_gate"], lw["w_in"], lw["w_branch"], lw["w_out"],
                    layer=l, tm=db, transposed_attention=False)
        ys = _ffn(ys, lw["ln2"], lw["f2_w1"], lw["f2_w3"], lw["f2_w2"], gf,
                  layer=l, final_norm=last, tm=db, tf=FFN_TILE_F)
        kv_shape = (db, 1, N_HEADS, HEAD_DIM)
        outs["s_mk"].append(ka.reshape(kv_shape)); outs["s_mv"].append(va.reshape(kv_shape))
        outs["s_sk"].append(kc.reshape(kv_shape)); outs["s_sv"].append(vc.reshape(kv_shape))
        outs["s_h"].append(h_new); outs["s_cv"].append(jnp.transpose(conv_new, (1, 0, 2)))

    st = {k: jnp.stack(v) for k, v in outs.items()}
    for k in ("p_mk", "p_mv", "p_sk", "p_sv"):
        st[k] = jnp.transpose(st[k].reshape(depth, bp, N_HEADS, HEAD_DIM, seq), (0, 1, 4, 2, 3))
    return (yp.reshape(bp, seq, d), ys.reshape(db, 1, d),
            st["p_mk"], st["p_mv"], st["p_sk"], st["p_sv"], st["p_h"], st["p_cv"],
            st["s_mk"], st["s_mv"], st["s_sk"], st["s_sv"], st["s_h"], st["s_cv"])
```

```python
import functools
import math

import jax
import jax.numpy as jnp
from jax import lax
from jax.experimental import pallas as pl
from jax.experimental.pallas import tpu as pltpu

F32 = jnp.float32
BF16 = jnp.bfloat16

HEAD_DIM = 64
N_HEADS = 8
ATT_W = N_HEADS * HEAD_DIM
LANES = 128
HEADS_PER_TILE = LANES // HEAD_DIM
N_PAIRS = ATT_W // LANES
MOBA_BLOCK = 256
MOBA_TOPK = 3
PAGE_SIZE = 128
PAGES_PER_MOBA_BLOCK = MOBA_BLOCK // PAGE_SIZE
PAGES_PER_STEP = 16
LRU_W = 512
LRU_BLOCKS = 8
CONV_W = 4
RG_C = 8.0
N_BUCKETS = 32
MAX_DIST = 128
RMS_EPS = 1e-6
NEG = -1e30
SCALE = HEAD_DIM ** -0.5
VMEM_LIMIT = 56 * 1024 * 1024

_NT = (((1,), (1,)), ((), ()))


def _params(*sem):
    return pltpu.CompilerParams(dimension_semantics=sem, vmem_limit_bytes=VMEM_LIMIT)


def _rms(x, g):
    return x * lax.rsqrt(jnp.mean(x * x, axis=-1, keepdims=True) + RMS_EPS) * g


def _softplus(z):
    return jnp.maximum(z, 0.0) + jnp.log1p(jnp.exp(-jnp.abs(z)))


def _softplus_abs(z):
    return jnp.maximum(z, 0.0) + jnp.log(1.0 + jnp.exp(-jnp.abs(z)))


def _one_minus_exp_2x(x):
    t = jnp.tanh(x)
    return -2.0 * t / (1.0 - t)


def _dot_nt(a, b):
    return lax.dot_general(a.astype(BF16), b.astype(BF16), _NT, preferred_element_type=F32)


def _ffn_kernel(x_ref, g_ref, w1_ref, w3_ref, w2_ref, gf_ref, o_ref, h_ref, acc_ref, *, final_norm):
    j = pl.program_id(1)

    @pl.when(j == 0)
    def _():
        h_ref[...] = _rms(x_ref[...], g_ref[...]).astype(BF16)
        acc_ref[...] = jnp.zeros_like(acc_ref)

    h = h_ref[...]
    a = jnp.dot(h, w1_ref[...], preferred_element_type=F32)
    b = jnp.dot(h, w3_ref[...], preferred_element_type=F32)
    u = a * jax.nn.sigmoid(a) * b
    acc_ref[...] += jnp.dot(u.astype(BF16), w2_ref[...], preferred_element_type=F32)

    @pl.when(j == pl.num_programs(1) - 1)
    def _():
        y = x_ref[...] + 0.5 * acc_ref[...]
        if final_norm:
            y = _rms(y, gf_ref[...])
        o_ref[...] = y


def _ffn(x, g, w1, w3, w2, gf, *, layer, final_norm, tm, tf):
    n, d = x.shape
    dff = w1.shape[2]
    return pl.pallas_call(
        functools.partial(_ffn_kernel, final_norm=final_norm),
        grid=(n // tm, dff // tf),
        in_specs=[
            pl.BlockSpec((tm, d), lambda i, j: (i, 0)),
            pl.BlockSpec((1, d), lambda i, j: (0, 0)),
            pl.BlockSpec((None, d, tf), lambda i, j: (layer, 0, j)),
            pl.BlockSpec((None, d, tf), lambda i, j: (layer, 0, j)),
            pl.BlockSpec((None, tf, d), lambda i, j: (layer, j, 0)),
            pl.BlockSpec((1, d), lambda i, j: (0, 0)),
        ],
        out_specs=pl.BlockSpec((tm, d), lambda i, j: (i, 0)),
        out_shape=jax.ShapeDtypeStruct((n, d), F32),
        scratch_shapes=[pltpu.VMEM((tm, d), BF16), pltpu.VMEM((tm, d), F32)],
        compiler_params=_params("parallel", "arbitrary"),
        name="ffn",
    )(x, g, w1, w3, w2, gf)


N_IN_SPLIT = 8
K_SLOTS = (1, 6)
V_SLOTS = (2, 7)


def _inproj_kernel(x_ref, g_ref, w_ref, *refs, prompt_layouts):
    outs, extra, h_ref = refs[:N_IN_SPLIT], refs[N_IN_SPLIT:-1], refs[-1]
    j = pl.program_id(1)

    @pl.when(j == 0)
    def _():
        h_ref[...] = _rms(x_ref[...], g_ref[...]).astype(BF16)

    for k in range(N_IN_SPLIT // 2):
        @pl.when(j == k)
        def _(k=k):
            p = jnp.dot(h_ref[...], w_ref[...], preferred_element_type=F32)
            for slot in (2 * k, 2 * k + 1):
                part = p[:, (slot % 2) * ATT_W:(slot % 2 + 1) * ATT_W]
                if not prompt_layouts or slot not in K_SLOTS + V_SLOTS:
                    outs[slot][...] = part
                    continue
                part_t = part.T
                outs[slot][...] = part_t
                if slot in K_SLOTS:
                    extra[2 * K_SLOTS.index(slot)][...] = part.astype(BF16)
                    if slot == K_SLOTS[0]:
                        kmean_ref = extra[-1]
                        for blk in range(kmean_ref.shape[0]):
                            kmean_ref[blk:blk + 1, :] = jnp.mean(
                                part[blk * MOBA_BLOCK:(blk + 1) * MOBA_BLOCK, :], axis=0, keepdims=True)
                else:
                    vt_ref = extra[2 * V_SLOTS.index(slot) + 1]
                    for blk in range(vt_ref.shape[0]):
                        vt_ref[blk] = part_t[:, blk * MOBA_BLOCK:(blk + 1) * MOBA_BLOCK].astype(BF16)


def _inproj(x, g, w_in, *, layer, tm, seq=None):
    n, d = x.shape
    tn = 2 * ATT_W
    out_specs = [pl.BlockSpec((tm, ATT_W), lambda i, j: (i, 0)) for _ in range(N_IN_SPLIT)]
    out_shape = [jax.ShapeDtypeStruct((n, ATT_W), F32) for _ in range(N_IN_SPLIT)]
    if seq is not None:
        tiles_per_seq = seq // tm
        blocks_per_tile = tm // MOBA_BLOCK
        for slot in K_SLOTS + V_SLOTS:
            out_specs[slot] = pl.BlockSpec((None, ATT_W, tm),
                                           lambda i, j: (i // tiles_per_seq, 0, i % tiles_per_seq))
            out_shape[slot] = jax.ShapeDtypeStruct((n // seq, ATT_W, seq), F32)
        for _ in range(2):
            out_specs.append(pl.BlockSpec((tm, ATT_W), lambda i, j: (i, 0)))
            out_shape.append(jax.ShapeDtypeStruct((n, ATT_W), BF16))
            out_specs.append(pl.BlockSpec((blocks_per_tile, ATT_W, MOBA_BLOCK), lambda i, j: (i, 0, 0)))
            out_shape.append(jax.ShapeDtypeStruct((n // MOBA_BLOCK, ATT_W, MOBA_BLOCK), BF16))
        out_specs.append(pl.BlockSpec((None, blocks_per_tile, ATT_W), lambda i, j: (i, 0, 0)))
        out_shape.append(jax.ShapeDtypeStruct((n // tm, blocks_per_tile, ATT_W), F32))
    return pl.pallas_call(
        functools.partial(_inproj_kernel, prompt_layouts=seq is not None),
        grid=(n // tm, N_IN_SPLIT * ATT_W // tn),
        in_specs=[
            pl.BlockSpec((tm, d), lambda i, j: (i, 0)),
            pl.BlockSpec((1, d), lambda i, j: (0, 0)),
            pl.BlockSpec((None, d, tn), lambda i, j: (layer, 0, j)),
        ],
        out_specs=out_specs,
        out_shape=out_shape,
        scratch_shapes=[pltpu.VMEM((tm, d), BF16)],
        compiler_params=_params("parallel", "arbitrary"),
        name="inproj",
    )(x, g, w_in)


N_BRANCH = 3


def _merge_kernel(oa_ref, ob_ref, oc_ref, x_ref, g_ref, bg_ref, *refs, transposed_attention):
    wg_refs, (wb_ref, wo_ref, o_ref) = refs[:N_BRANCH], refs[N_BRANCH:]
    x = x_ref[...]
    h = _rms(x, g_ref[...]).astype(BF16)
    merged = None
    for n, o_n in enumerate((oa_ref, ob_ref, oc_ref)):
        branch = o_n[...]
        if transposed_attention and n != 1:
            branch = branch.T
        proj = jnp.dot(branch.astype(BF16), wb_ref[n], preferred_element_type=F32)
        logits = jnp.dot(h, wg_refs[n][...], preferred_element_type=F32)
        gate = jax.nn.sigmoid(logits + bg_ref[n:n + 1, :])
        merged = gate * proj if merged is None else merged + gate * proj
    o_ref[...] = x + jnp.dot(merged.astype(BF16), wo_ref[...], preferred_element_type=F32)


def _merge(oa, ob, oc, x, g, bg, w_in, wb, wo, *, layer, tm, transposed_attention):
    n, d = x.shape
    first_gate_tile = w_in.shape[2] // d - N_BRANCH
    row = lambda w: pl.BlockSpec((tm, w), lambda i: (i, 0))
    att = pl.BlockSpec((None, ATT_W, tm), lambda i: (i, 0, 0)) if transposed_attention else row(ATT_W)
    gate_w = [pl.BlockSpec((None, d, d), lambda i, t=first_gate_tile + k: (layer, 0, t))
              for k in range(N_BRANCH)]
    return pl.pallas_call(
        functools.partial(_merge_kernel, transposed_attention=transposed_attention),
        grid=(n // tm,),
        in_specs=[att, row(ATT_W), att, row(d),
                  pl.BlockSpec((1, d), lambda i: (0, 0)),
                  pl.BlockSpec(bg.shape, lambda i: (0, 0)),
                  *gate_w,
                  pl.BlockSpec((None,) + wb.shape[1:], lambda i: (layer, 0, 0, 0)),
                  pl.BlockSpec((None,) + wo.shape[1:], lambda i: (layer, 0, 0))],
        out_specs=row(d),
        out_shape=jax.ShapeDtypeStruct((n, d), F32),
        compiler_params=_params("parallel"),
        name="merge",
    )(oa, ob, oc, x, g, bg, *([w_in] * N_BRANCH), wb, wo)


def _rel_bucket(dist):
    max_exact = N_BUCKETS // 2
    d = jnp.maximum(dist, 0)
    df = jnp.maximum(d, 1).astype(F32)
    large = max_exact + (jnp.log(df / max_exact) / math.log(MAX_DIST / max_exact)
                         * (N_BUCKETS - max_exact)).astype(jnp.int32)
    large = jnp.minimum(large, N_BUCKETS - 1)
    return jnp.where(d < max_exact, d, large)


def _bias_tables(rel_bias):
    period = 4 * MOBA_BLOCK
    k = jnp.arange(period, dtype=jnp.int32)
    k = jnp.where(k < period // 2, k, k - period)
    u = rel_bias[_rel_bucket(jnp.maximum(MOBA_BLOCK + k, 0))].T
    n_keys = 2 * MOBA_BLOCK
    toeplitz = jnp.tile(u, (1, n_keys))[:, :n_keys * (period - 1)]
    toeplitz = toeplitz.reshape(N_HEADS, n_keys, period - 1)[:, :, :MOBA_BLOCK]
    heads_per_step = STEP_W // HEAD_DIM
    n_steps = N_HEADS // heads_per_step
    toeplitz = toeplitz.reshape(n_steps, heads_per_step, n_keys, MOBA_BLOCK)
    toeplitz = jnp.transpose(toeplitz, (0, 2, 1, 3)).reshape(n_steps, n_keys, STEP_Q)
    far = rel_bias[_rel_bucket(jnp.int32(2 * MOBA_BLOCK))]
    far_pair = jnp.repeat(far.reshape(n_steps, 1, heads_per_step), MOBA_BLOCK, axis=2)
    dec = rel_bias[_rel_bucket(MOBA_BLOCK - jnp.arange(MOBA_BLOCK, dtype=jnp.int32))].T
    own0 = rel_bias[_rel_bucket(jnp.int32(0))][:, None]
    return toeplitz, far_pair, dec, far[:, None], own0


PAIR_Q = HEADS_PER_TILE * MOBA_BLOCK
TILES_PER_STEP = 4
STEP_W = TILES_PER_STEP * LANES
STEP_Q = TILES_PER_STEP * PAIR_Q


def _stacked_queries(q):
    lane = lax.broadcasted_iota(jnp.int32, (1, LANES), 1)
    stacks = []
    for g in range(TILES_PER_STEP):
        qs = q[:, g * LANES:(g + 1) * LANES] * SCALE
        stacks.append(jnp.concatenate([jnp.where((lane // HEAD_DIM) == hh, qs, 0.0).astype(BF16)
                                       for hh in range(HEADS_PER_TILE)], axis=0))
    return stacks


def _scores(k_blk, qcat):
    return jnp.concatenate([_dot_nt(k_blk[:, g * LANES:(g + 1) * LANES], qcat[g])
                            for g in range(TILES_PER_STEP)], axis=1)


def _values(vt_blk, w):
    return jnp.concatenate([jnp.dot(vt_blk[g * LANES:(g + 1) * LANES, :], w[:, g * PAIR_Q:(g + 1) * PAIR_Q],
                                    preferred_element_type=F32)
                            for g in range(TILES_PER_STEP)], axis=0)


def _per_head_rows(stat):
    return jnp.concatenate([jnp.broadcast_to(stat[:, g * PAIR_Q:(g + 1) * PAIR_Q], (LANES, PAIR_Q))
                            for g in range(TILES_PER_STEP)], axis=0)


def _pair_corners(acc):
    rows = []
    for r in range(STEP_W // HEAD_DIM):
        hh = r % HEADS_PER_TILE
        rows.append(acc[r * HEAD_DIM:(r + 1) * HEAD_DIM, hh * MOBA_BLOCK:(hh + 1) * MOBA_BLOCK])
    return jnp.concatenate(rows, axis=0)


def _moba_prompt_kernel(q_ref, kmean_ref, kb_ref, vt_ref, tb_ref, far_ref, o_ref, sel_ref):
    i = pl.program_id(2)
    nb = kb_ref.shape[0] // MOBA_BLOCK
    q = q_ref[...]
    qcat = _stacked_queries(q)
    lane = lax.broadcasted_iota(jnp.int32, (1, LANES), 1)
    blk_row = lax.broadcasted_iota(jnp.int32, (nb, 1), 0)

    gate = jnp.concatenate(
        [lax.dot_general(jnp.where((lane // HEAD_DIM) == hh, kmean_ref[:, g * LANES:(g + 1) * LANES], 0.0),
                         q[:, g * LANES:(g + 1) * LANES], _NT,
                         precision=lax.Precision.HIGHEST, preferred_element_type=F32)
         for g in range(TILES_PER_STEP) for hh in range(HEADS_PER_TILE)], axis=1)
    gate = jnp.where(blk_row < i, gate, NEG)
    rank = jnp.zeros_like(gate)
    for m in range(nb):
        gm = gate[m:m + 1, :]
        ahead = (gm > gate) | ((gm == gate) & (blk_row > m))
        rank = rank + ahead.astype(F32)
    sel_ref[...] = ((rank < MOBA_TOPK) & (blk_row < i)).astype(F32)

    def block(n, bias, keep, m_run, l_run, acc):
        start = pl.multiple_of(n * MOBA_BLOCK, MOBA_BLOCK)
        s = _scores(kb_ref[pl.ds(start, MOBA_BLOCK), :], qcat) + bias
        s = jnp.where(keep, s, NEG)
        m_new = jnp.max(s, axis=0, keepdims=True)
        if m_run is not None:
            m_new = jnp.maximum(m_run, m_new)
        p = jnp.exp(s - m_new)
        l_new = jnp.sum(p, axis=0, keepdims=True)
        pv = _values(vt_ref[n], p.astype(BF16))
        if m_run is None:
            return m_new, l_new, pv
        alpha = jnp.exp(m_run - m_new)
        return m_new, alpha * l_run + l_new, _per_head_rows(alpha) * acc + pv

    key_i = lax.broadcasted_iota(jnp.int32, (MOBA_BLOCK, STEP_Q), 0)
    qry_i = lax.broadcasted_iota(jnp.int32, (MOBA_BLOCK, STEP_Q), 1) % MOBA_BLOCK
    state0 = block(i, tb_ref[MOBA_BLOCK:, :], key_i <= qry_i, None, None, None)

    def body(n, state):
        bias = jnp.where(n == i - 1, tb_ref[:MOBA_BLOCK, :], far_ref[...])
        return block(n, bias, sel_ref[pl.ds(n, 1), :] > 0.5, *state)

    _, l_fin, acc_fin = lax.fori_loop(0, i, body, state0)
    o_ref[...] = _pair_corners(acc_fin / _per_head_rows(l_fin))


def _moba_prompt(q, k_means, k_bf, vt, toeplitz, far, *, batch, seq):
    nb = seq // MOBA_BLOCK
    return pl.pallas_call(
        _moba_prompt_kernel,
        grid=(batch, ATT_W // STEP_W, nb),
        in_specs=[
            pl.BlockSpec((MOBA_BLOCK, STEP_W), lambda b, p, i: (b * nb + i, p)),
            pl.BlockSpec((nb, STEP_W), lambda b, p, i: (b, p)),
            pl.BlockSpec((seq, STEP_W), lambda b, p, i: (b, p)),
            pl.BlockSpec((nb, STEP_W, MOBA_BLOCK), lambda b, p, i: (b, p, 0)),
            pl.BlockSpec((None, 2 * MOBA_BLOCK, STEP_Q), lambda b, p, i: (p, 0, 0)),
            pl.BlockSpec((None, 1, STEP_Q), lambda b, p, i: (p, 0, 0)),
        ],
        out_specs=pl.BlockSpec((None, STEP_W, MOBA_BLOCK), lambda b, p, i: (b * nb + i, p, 0)),
        out_shape=jax.ShapeDtypeStruct((batch * nb, ATT_W, MOBA_BLOCK), F32),
        scratch_shapes=[pltpu.VMEM((nb, STEP_Q), F32)],
        compiler_params=_params("parallel", "parallel", "arbitrary"),
        name="moba_prompt",
    )(q, k_means, k_bf, vt, toeplitz, far)


SB_BLOCK = 256
F32_EXP_UNDERFLOW = -105.0


def _sb_prompt_kernel(q_ref, kb_ref, vt_ref, o_ref):
    i = pl.program_id(2)
    qcat = _stacked_queries(q_ref[...])
    row_i = lax.broadcasted_iota(jnp.int32, (SB_BLOCK, SB_BLOCK), 0)
    col_i = lax.broadcasted_iota(jnp.int32, (SB_BLOCK, SB_BLOCK), 1)
    later = (row_i < col_i).astype(BF16)

    def block(n, mask, carry):
        start = pl.multiple_of(n * SB_BLOCK, SB_BLOCK)
        z = _scores(kb_ref[pl.ds(start, SB_BLOCK), :], qcat)
        sp = _softplus_abs(z)
        log_keep = -sp if mask is None else jnp.where(mask, -sp, 0.0)
        hi = log_keep.astype(BF16)
        lo = (log_keep - hi.astype(F32)).astype(BF16)
        both = jnp.dot(later, jnp.concatenate([hi, lo], axis=1), preferred_element_type=F32)
        between = both[:, :STEP_Q] + both[:, STEP_Q:]
        if carry is not None:
            between = between + carry
        w = jnp.exp(z - sp + between)
        if mask is not None:
            w = jnp.where(mask, w, 0.0)
        pv = _values(vt_ref[n], w.astype(BF16))
        return jnp.sum(log_keep, axis=0, keepdims=True), pv

    key_i = lax.broadcasted_iota(jnp.int32, (SB_BLOCK, STEP_Q), 0)
    qry_i = lax.broadcasted_iota(jnp.int32, (SB_BLOCK, STEP_Q), 1) % SB_BLOCK
    state0 = block(i, key_i < qry_i, None)

    def more(state):
        t, carry, _ = state
        return jnp.logical_and(t < i, jnp.max(carry) > F32_EXP_UNDERFLOW)

    def body(state):
        t, carry, acc = state
        kept, pv = block(i - 1 - t, None, carry)
        return t + 1, carry + kept, acc + pv

    _, _, acc_fin = lax.while_loop(more, body, (jnp.int32(0),) + state0)
    o_ref[...] = _pair_corners(acc_fin)


def _sb_prompt(q, k_bf, vt, *, batch, seq):
    nb = seq // SB_BLOCK
    return pl.pallas_call(
        _sb_prompt_kernel,
        grid=(batch, ATT_W // STEP_W, nb),
        in_specs=[
            pl.BlockSpec((SB_BLOCK, STEP_W), lambda b, p, i: (b * nb + i, p)),
            pl.BlockSpec((seq, STEP_W), lambda b, p, i: (b, p)),
            pl.BlockSpec((nb, STEP_W, SB_BLOCK), lambda b, p, i: (b, p, 0)),
        ],
        out_specs=pl.BlockSpec((None, STEP_W, SB_BLOCK), lambda b, p, i: (b * nb + i, p, 0)),
        out_shape=jax.ShapeDtypeStruct((batch * nb, ATT_W, SB_BLOCK), F32),
        compiler_params=_params("parallel", "parallel", "arbitrary"),
        name="sb_prompt",
    )(q, k_bf, vt)


def _lru_gates(xc, wa_ref, ba_ref, wx_ref, bx_ref, lam_ref):
    xc_bf = xc.astype(BF16)
    r = jax.nn.sigmoid(jnp.dot(xc_bf, wa_ref[...], preferred_element_type=F32) + ba_ref[...])
    gi = jax.nn.sigmoid(jnp.dot(xc_bf, wx_ref[...], preferred_element_type=F32) + bx_ref[...])
    log_a = -RG_C * r * _softplus(-lam_ref[...])
    a = jnp.exp(log_a)
    b = jnp.sqrt(_one_minus_exp_2x(log_a)) * (gi * xc)
    return a, b


def _lru_prompt_kernel(x_ref, g_ref, buf_ref, h0_ref, cw_ref, cb_ref, wa_ref, ba_ref, wx_ref, bx_ref,
                       lam_ref, y_ref, hlast_ref, conv_ref, tail_ref, hc_ref):
    t = pl.program_id(1)
    tc = x_ref.shape[0]
    tail_rows = tail_ref.shape[0]

    @pl.when(t == 0)
    def _():
        tail_ref[...] = jnp.zeros_like(tail_ref)
        tail_ref[tail_rows - (CONV_W - 1):, :] = buf_ref[...]
        hc_ref[...] = h0_ref[...]

    x = x_ref[...]
    xe = jnp.concatenate([tail_ref[...], x], axis=0)
    xc = cb_ref[...] + cw_ref[CONV_W - 1:CONV_W, :] * x
    for k in range(1, CONV_W):
        xc = xc + cw_ref[CONV_W - 1 - k:CONV_W - k, :] * pltpu.roll(xe, k, 0)[tail_rows:, :]
    a, b = _lru_gates(xc, wa_ref, ba_ref, wx_ref, bx_ref, lam_ref)

    row = lax.broadcasted_iota(jnp.int32, (tc, 1), 0)
    k = 1
    while k < tc:
        a_sh = jnp.where(row >= k, pltpu.roll(a, k, 0), 1.0)
        b_sh = jnp.where(row >= k, pltpu.roll(b, k, 0), 0.0)
        b = a * b_sh + b
        a = a * a_sh
        k *= 2
    h = a * hc_ref[...] + b
    y_ref[...] = h * jax.nn.gelu(g_ref[...], approximate=True)
    hc_ref[...] = h[tc - 1:tc, :]
    tail_ref[...] = x[tc - tail_rows:, :]

    @pl.when(t == pl.num_programs(1) - 1)
    def _():
        hlast_ref[...] = h[tc - 1:tc, :]
        conv_ref[...] = x[tc - (CONV_W - 1):, :]


def _lru_prompt(x, g, buf, h0, lw, *, batch, seq, tc):
    nt = seq // tc
    w = x.shape[1]
    vec = pl.BlockSpec((1, w), lambda b, t: (0, 0))
    mat = pl.BlockSpec((w, w), lambda b, t: (0, 0))
    return pl.pallas_call(
        _lru_prompt_kernel,
        grid=(batch, nt),
        in_specs=[
            pl.BlockSpec((tc, w), lambda b, t: (b * nt + t, 0)),
            pl.BlockSpec((tc, w), lambda b, t: (b * nt + t, 0)),
            pl.BlockSpec((None, CONV_W - 1, w), lambda b, t: (b, 0, 0)),
            pl.BlockSpec((None, 1, w), lambda b, t: (b, 0, 0)),
            pl.BlockSpec((CONV_W, w), lambda b, t: (0, 0)),
            vec, mat, vec, mat, vec, vec,
        ],
        out_specs=[
            pl.BlockSpec((tc, w), lambda b, t: (b * nt + t, 0)),
            pl.BlockSpec((None, 1, w), lambda b, t: (b, 0, 0)),
            pl.BlockSpec((None, CONV_W - 1, w), lambda b, t: (b, 0, 0)),
        ],
        out_shape=[jax.ShapeDtypeStruct(x.shape, F32),
                   jax.ShapeDtypeStruct((batch, 1, w), F32),
                   jax.ShapeDtypeStruct((batch, CONV_W - 1, w), F32)],
        scratch_shapes=[pltpu.VMEM((8, w), F32), pltpu.VMEM((1, w), F32)],
        compiler_params=_params("parallel", "arbitrary"),
        name="lru_prompt",
    )(x, g, buf, h0, lw["conv_w"], lw["conv_b"], lw["wa"], lw["ba"], lw["wx"], lw["bx"], lw["lam"])


def _lru_step_kernel(x_ref, g_ref, buf_ref, h0_ref, cw_ref, cb_ref, wa_ref, ba_ref, wx_ref, bx_ref,
                     lam_ref, y_ref, h_ref, conv_ref):
    x = x_ref[...]
    xc = cb_ref[...] + cw_ref[CONV_W - 1:CONV_W, :] * x
    for k in range(CONV_W - 1):
        xc = xc + cw_ref[k:k + 1, :] * buf_ref[k]
    a, b = _lru_gates(xc, wa_ref, ba_ref, wx_ref, bx_ref, lam_ref)
    h = a * h0_ref[...] + b
    y_ref[...] = h * jax.nn.gelu(g_ref[...], approximate=True)
    h_ref[...] = h
    for k in range(CONV_W - 2):
        conv_ref[k] = buf_ref[k + 1]
    conv_ref[CONV_W - 2] = x


def _lru_step(x, g, buf, h0, lw):
    n, w = x.shape
    return pl.pallas_call(
        _lru_step_kernel,
        out_shape=[jax.ShapeDtypeStruct((n, w), F32), jax.ShapeDtypeStruct((n, w), F32),
                   jax.ShapeDtypeStruct((CONV_W - 1, n, w), F32)],
        compiler_params=pltpu.CompilerParams(vmem_limit_bytes=VMEM_LIMIT),
        name="lru_step",
    )(x, g, buf, h0, lw["conv_w"], lw["conv_b"], lw["wa"], lw["ba"], lw["wx"], lw["bx"], lw["lam"])


def _page_scores(qb, kt_ref):
    return jnp.concatenate([jnp.sum(qb[h] * kt_ref[h], axis=0, keepdims=True)
                            for h in range(N_HEADS)], axis=0)


def _weighted_values(w, vt_ref):
    return jnp.stack([jnp.broadcast_to(w[h:h + 1, :], (HEAD_DIM, PAGE_SIZE)) * vt_ref[h]
                      for h in range(N_HEADS)], axis=0)


def _sum_positions(acc):
    ones = jnp.ones((N_HEADS, PAGE_SIZE), F32)
    rows = [lax.dot_general(ones, acc[h], _NT, precision=lax.Precision.HIGHEST,
                            preferred_element_type=F32)[:1] for h in range(N_HEADS)]
    return jnp.concatenate(rows, axis=0)


def _moba_decode_kernel(pt_ref, q_ref, q2_ref, kn_ref, vn_ref, dec_ref, far_ref, own_ref, *refs):
    del pt_ref
    n_pg = PAGES_PER_STEP
    k_pages, v_pages = refs[:n_pg], refs[n_pg:2 * n_pg]
    o_ref, gate_ref, m_ref, l_ref, acc_ref = refs[2 * n_pg:]
    c = pl.program_id(1)
    n_steps = pl.num_programs(1)
    blocks_per_step = n_pg // PAGES_PER_MOBA_BLOCK
    qb = jnp.broadcast_to(q_ref[...] * SCALE, (N_HEADS, HEAD_DIM, PAGE_SIZE))
    lane = lax.broadcasted_iota(jnp.int32, (1, LANES), 1)

    @pl.when(c == 0)
    def _():
        gate_ref[...] = jnp.full_like(gate_ref, NEG)
        m_ref[...] = jnp.full_like(m_ref, NEG)
        l_ref[...] = jnp.zeros_like(l_ref)

    for j in range(blocks_per_step):
        n = c * blocks_per_step + j
        pages = range(PAGES_PER_MOBA_BLOCK * j, PAGES_PER_MOBA_BLOCK * (j + 1))
        z = [_page_scores(qb, k_pages[pg]) for pg in pages]
        g = sum(jnp.sum(zt, axis=-1, keepdims=True) for zt in z) * (1.0 / (SCALE * MOBA_BLOCK))
        newest = n == n_steps * blocks_per_step - 1
        s = [zt + jnp.where(newest, dec_ref[:, t * PAGE_SIZE:(t + 1) * PAGE_SIZE], far_ref[...])
             for t, zt in enumerate(z)]
        m = functools.reduce(jnp.maximum, [jnp.max(st, axis=-1, keepdims=True) for st in s])
        p = [jnp.exp(st - m) for st in s]
        l = sum(jnp.sum(pt, axis=-1, keepdims=True) for pt in p)
        acc_ref[n] = sum(_weighted_values(pt, v_pages[pg]) for pt, pg in zip(p, pages))
        gate_ref[...] = jnp.where(lane == n, g, gate_ref[...])
        m_ref[...] = jnp.where(lane == n, m, m_ref[...])
        l_ref[...] = jnp.where(lane == n, l, l_ref[...])

    @pl.when(c == n_steps - 1)
    def _():
        nb = n_steps * blocks_per_step
        gate = gate_ref[...]
        rank = jnp.zeros_like(gate)
        for mm in range(nb):
            gm = gate[:, mm:mm + 1]
            ahead = (gm > gate) | ((gm == gate) & (lane > mm))
            rank = rank + ahead.astype(F32)
        sel = (rank < MOBA_TOPK) & (lane < nb)
        s_own = jnp.sum(q2_ref[...] * SCALE * kn_ref[...], axis=-1, keepdims=True) + own_ref[...]
        m_sel = jnp.where(sel, m_ref[...], NEG)
        m_tot = jnp.maximum(jnp.max(m_sel, axis=-1, keepdims=True), s_own)
        wgt = jnp.where(sel, jnp.exp(m_sel - m_tot), 0.0)
        w_own = jnp.exp(s_own - m_tot)
        l_tot = jnp.sum(wgt * l_ref[...], axis=-1, keepdims=True) + w_own
        merged = []
        for h in range(N_HEADS):
            tot = jnp.zeros((HEAD_DIM, PAGE_SIZE), F32)
            for mm in range(nb):
                tot = tot + jnp.broadcast_to(wgt[h:h + 1, mm:mm + 1], (HEAD_DIM, PAGE_SIZE)) * acc_ref[mm, h]
            merged.append(tot)
        o = w_own * vn_ref[...] + _sum_positions(merged)
        o_ref[...] = o / l_tot


def _page_spec(layer, slot):
    def index_map(b, c, pt):
        return (layer, pt[b, c * PAGES_PER_STEP + slot], 0, 0, 0)
    return pl.BlockSpec((None, None, N_HEADS, HEAD_DIM, PAGE_SIZE), index_map)


def _moba_decode(page_table, q, k_new, v_new, cache_k, cache_v, dec, far, own0, *, layer):
    batch, n_pages = page_table.shape
    n_steps = n_pages // PAGES_PER_STEP
    nb = n_pages // PAGES_PER_MOBA_BLOCK
    heads = pl.BlockSpec((None, N_HEADS, HEAD_DIM), lambda b, c, pt: (b, 0, 0))
    column = pl.BlockSpec((None, N_HEADS, HEAD_DIM, 1), lambda b, c, pt: (b, 0, 0, 0))
    const = lambda a: pl.BlockSpec(a.shape, lambda b, c, pt: (0,) * a.ndim)
    grid_spec = pltpu.PrefetchScalarGridSpec(
        num_scalar_prefetch=1,
        grid=(batch, n_steps),
        in_specs=[column, heads, heads, heads, const(dec), const(far), const(own0)]
        + [_page_spec(layer, s) for s in range(PAGES_PER_STEP)]
        + [_page_spec(layer, s) for s in range(PAGES_PER_STEP)],
        out_specs=heads,
        scratch_shapes=[pltpu.VMEM((N_HEADS, LANES), F32), pltpu.VMEM((N_HEADS, LANES), F32),
                        pltpu.VMEM((N_HEADS, LANES), F32),
                        pltpu.VMEM((nb, N_HEADS, HEAD_DIM, PAGE_SIZE), F32)],
    )
    return pl.pallas_call(
        _moba_decode_kernel,
        grid_spec=grid_spec,
        out_shape=jax.ShapeDtypeStruct((batch, N_HEADS, HEAD_DIM), F32),
        compiler_params=_params("parallel", "arbitrary"),
        name="moba_decode",
    )(page_table, q[..., None], q, k_new, v_new, dec, far, own0,
      *([cache_k] * PAGES_PER_STEP), *([cache_v] * PAGES_PER_STEP))


SB_DECODE_SLOTS = 4


def _sb_decode_kernel(pt_ref, q_ref, k_hbm, v_hbm, o_ref, kbuf, vbuf, sem, *, layer):
    b = pl.program_id(0)
    n_pages = pt_ref.shape[1]
    ahead = SB_DECODE_SLOTS - 1
    qb = jnp.broadcast_to(q_ref[...] * SCALE, (N_HEADS, HEAD_DIM, PAGE_SIZE))
    lane = lax.broadcasted_iota(jnp.int32, (1, PAGE_SIZE), 1)

    def page_copies(j):
        page = pt_ref[b, n_pages - 1 - j]
        slot = j % SB_DECODE_SLOTS
        return (pltpu.make_async_copy(k_hbm.at[layer, page], kbuf.at[slot], sem.at[0, slot]),
                pltpu.make_async_copy(v_hbm.at[layer, page], vbuf.at[slot], sem.at[1, slot]))

    def start(j):
        for copy in page_copies(j):
            copy.start()

    def wait(j):
        for copy in page_copies(j):
            copy.wait()

    for j in range(ahead):
        start(j)

    def more(state):
        j, carry, _ = state
        return jnp.logical_and(j < n_pages, jnp.max(carry) > F32_EXP_UNDERFLOW)

    def body(state):
        j, carry, acc = state
        slot = j % SB_DECODE_SLOTS
        wait(j)

        @pl.when(j + ahead < n_pages)
        def _():
            start(j + ahead)

        z = _page_scores(qb, kbuf.at[slot])
        sp = _softplus_abs(z)
        log_keep = -sp
        incl = log_keep
        shift = 1
        while shift < PAGE_SIZE:
            moved = pltpu.roll(incl, PAGE_SIZE - shift, 1)
            incl = incl + jnp.where(lane < PAGE_SIZE - shift, moved, 0.0)
            shift *= 2
        w = jnp.exp(z - sp + incl - log_keep + carry)
        acc = acc + _weighted_values(w, vbuf.at[slot])
        return j + 1, carry + jnp.sum(log_keep, axis=-1, keepdims=True), acc

    state0 = (jnp.int32(0), jnp.zeros((N_HEADS, 1), F32), jnp.zeros((N_HEADS, HEAD_DIM, PAGE_SIZE), F32))
    j_end, _, acc = lax.while_loop(more, body, state0)

    for k in range(ahead):
        @pl.when(j_end + k < n_pages)
        def _(k=k):
            wait(j_end + k)

    o_ref[...] = _sum_positions(acc)


def _sb_decode(page_table, q, cache_k, cache_v, *, layer):
    batch, n_pages = page_table.shape
    assert n_pages >= SB_DECODE_SLOTS - 1
    page_buffer = pltpu.VMEM((SB_DECODE_SLOTS, N_HEADS, HEAD_DIM, PAGE_SIZE), F32)
    grid_spec = pltpu.PrefetchScalarGridSpec(
        num_scalar_prefetch=1,
        grid=(batch,),
        in_specs=[pl.BlockSpec((None, N_HEADS, HEAD_DIM, 1), lambda b, pt: (b, 0, 0, 0)),
                  pl.BlockSpec(memory_space=pl.ANY),
                  pl.BlockSpec(memory_space=pl.ANY)],
        out_specs=pl.BlockSpec((None, N_HEADS, HEAD_DIM), lambda b, pt: (b, 0, 0)),
        scratch_shapes=[page_buffer, page_buffer, pltpu.SemaphoreType.DMA((2, SB_DECODE_SLOTS))],
    )
    return pl.pallas_call(
        functools.partial(_sb_decode_kernel, layer=layer),
        grid_spec=grid_spec,
        out_shape=jax.ShapeDtypeStruct((batch, N_HEADS, HEAD_DIM), F32),
        compiler_params=_params("arbitrary"),
        name="sb_decode",
    )(page_table, q[..., None], cache_k, cache_v)


def _block_diag(w):
    nblk, bk, bj = w.shape
    eye = jnp.eye(nblk, dtype=w.dtype)
    return jnp.einsum("nkj,nm->nkmj", w, eye).reshape(nblk * bk, nblk * bj)


def _layer_weights(l, norm_ffn1, ffn1_w1, ffn1_w3, ffn1_w2, norm_mix, w_in, b_gate, conv_w, conv_b,
                   lru_wa, lru_ba, lru_wx, lru_bx, lru_lambda, w_branch, w_out, norm_ffn2,
                   ffn2_w1, ffn2_w3, ffn2_w2):
    vec = lambda a: a[l].reshape(1, -1)
    bf = lambda a: a.astype(BF16)
    return {
        "ln1": vec(norm_ffn1), "f1_w1": bf(ffn1_w1), "f1_w3": bf(ffn1_w3), "f1_w2": bf(ffn1_w2),
        "ln_mix": vec(norm_mix), "w_in": bf(w_in), "b_gate": b_gate[l],
        "lru": {"conv_w": conv_w[l], "conv_b": vec(conv_b),
                "wa": _block_diag(lru_wa[l]).astype(BF16), "ba": vec(lru_ba),
                "wx": _block_diag(lru_wx[l]).astype(BF16), "bx": vec(lru_bx), "lam": vec(lru_lambda)},
        "w_branch": bf(w_branch), "w_out": bf(w_out),
        "ln2": vec(norm_ffn2), "f2_w1": bf(ffn2_w1), "f2_w3": bf(ffn2_w3), "f2_w2": bf(ffn2_w2),
    }


FFN_TILE_F = 256


def kernel(x_prompt, x_sample, cache_moba_k, cache_moba_v, cache_sb_k, cache_sb_v, state_lru_h,
           state_lru_conv, page_table, rel_bias, norm_ffn1, ffn1_w1, ffn1_w3, ffn1_w2, norm_mix, w_in,
           b_gate, conv_w, conv_b, lru_wa, lru_ba, lru_wx, lru_bx, lru_lambda, w_branch, w_out,
           norm_ffn2, ffn2_w1, ffn2_w3, ffn2_w2, norm_final):
    bp, seq, d = x_prompt.shape
    db = x_sample.shape[0]
    depth = w_in.shape[0]
    n_phys = cache_moba_k.shape[1]
    toeplitz, far, dec, far_dec, own0 = _bias_tables(rel_bias)
    gf = norm_final.reshape(1, d)
    paged = lambda c: jnp.transpose(c, (0, 1, 3, 4, 2))
    cmk, cmv, csk, csv = paged(cache_moba_k), paged(cache_moba_v), paged(cache_sb_k), paged(cache_sb_v)

    yp = x_prompt.reshape(bp * seq, d)
    ys = x_sample.reshape(db, d)
    tm_p = 1024 if (bp * seq) % 1024 == 0 else MOBA_BLOCK
    zeros_buf = jnp.zeros((bp, CONV_W - 1, LRU_W), F32)
    zeros_h = jnp.zeros((bp, 1, LRU_W), F32)
    outs = {k: [] for k in ("p_mk", "p_mv", "p_sk", "p_sv", "p_h", "p_cv",
                            "s_mk", "s_mv", "s_sk", "s_sv", "s_h", "s_cv")}
    for l in range(depth):
        lw = _layer_weights(l, norm_ffn1, ffn1_w1, ffn1_w3, ffn1_w2, norm_mix, w_in, b_gate, conv_w,
                            conv_b, lru_wa, lru_ba, lru_wx, lru_bx, lru_lambda, w_branch, w_out,
                            norm_ffn2, ffn2_w1, ffn2_w3, ffn2_w2)
        last = l == depth - 1

        yp = _ffn(yp, lw["ln1"], lw["f1_w1"], lw["f1_w3"], lw["f1_w2"], gf,
                  layer=l, final_norm=False, tm=tm_p, tf=FFN_TILE_F)
        qa, ka, va, xb, gb, qc, kc, vc, ka_bf, vat, kc_bf, vct, ka_means = _inproj(
            yp, lw["ln_mix"], lw["w_in"], layer=l, tm=max(tm_p // 2, MOBA_BLOCK), seq=seq)
        o_a = _moba_prompt(qa, ka_means.reshape(bp * seq // MOBA_BLOCK, ATT_W), ka_bf, vat, toeplitz, far,
                           batch=bp, seq=seq)
        o_b, h_new, conv_new = _lru_prompt(xb, gb, zeros_buf, zeros_h, lw["lru"],
                                           batch=bp, seq=seq, tc=min(seq, 512))
        o_c = _sb_prompt(qc, kc_bf, vct, batch=bp, seq=seq)
        yp = _merge(o_a, o_b, o_c, yp, lw["ln_mix"], lw["b_gate"], lw["w_in"], lw["w_branch"], lw["w_out"],
                    layer=l, tm=MOBA_BLOCK, transposed_attention=True)
        yp = _ffn(yp, lw["ln2"], lw["f2_w1"], lw["f2_w3"], lw["f2_w2"], gf,
                  layer=l, final_norm=last, tm=tm_p, tf=FFN_TILE_F)
        outs["p_mk"].append(ka); outs["p_mv"].append(va)
        outs["p_sk"].append(kc); outs["p_sv"].append(vc)
        outs["p_h"].append(h_new.reshape(bp, LRU_W)); outs["p_cv"].append(conv_new)

        ys = _ffn(ys, lw["ln1"], lw["f1_w1"], lw["f1_w3"], lw["f1_w2"], gf,
                  layer=l, final_norm=False, tm=db, tf=FFN_TILE_F)
        qa, ka, va, xb, gb, qc, kc, vc = _inproj(ys, lw["ln_mix"], lw["w_in"], layer=l, tm=db)
        by_head = lambda a: a.reshape(db, N_HEADS, HEAD_DIM)
        o_a = _moba_decode(page_table, by_head(qa), by_head(ka), by_head(va), cmk, cmv,
                           dec, far_dec, own0, layer=l).reshape(db, ATT_W)
        o_b, h_new, conv_new = _lru_step(xb, gb, jnp.transpose(state_lru_conv[l], (1, 0, 2)),
                                         state_lru_h[l], lw["lru"])
        o_c = _sb_decode(page_table, by_head(qc), csk, csv, layer=l).reshape(db, ATT_W)
        ys = _merge(o_a, o_b, o_c, ys, lw["ln_mix"], lw["b_gate"], lw["w_in"], lw["w_branch"], lw["w_out"],
                    layer=l, tm=db, transposed_attention=False)
        ys = _ffn(ys, lw["ln2"], lw["f2_w1"], lw["f2_w3"], lw["f2_w2"], gf,
                  layer=l, final_norm=last, tm=db, tf=FFN_TILE_F)
        kv_shape = (db, 1, N_HEADS, HEAD_DIM)
        outs["s_mk"].append(ka.reshape(kv_shape)); outs["s_mv"].append(va.reshape(kv_shape))
        outs["s_sk"].append(kc.reshape(kv_shape)); outs["s_sv"].append(vc.reshape(kv_shape))
        outs["s_h"].append(h_new); outs["s_cv"].append(jnp.transpose(conv_new, (1, 0, 2)))

    st = {k: jnp.stack(v) for k, v in outs.items()}
    for k in ("p_mk", "p_mv", "p_sk", "p_sv"):
        st[k] = jnp.transpose(st[k].reshape(depth, bp, N_HEADS, HEAD_DIM, seq), (0, 1, 4, 2, 3))
    return (yp.reshape(bp, seq, d), ys.reshape(db, 1, d),
            st["p_mk"], st["p_mv"], st["p_sk"], st["p_sv"], st["p_h"], st["p_cv"],
            st["s_mk"], st["s_mv"], st["s_sk"], st["s_sv"], st["s_h"], st["s_cv"])
```

```python
import functools
import math

import jax
import jax.numpy as jnp
from jax import lax
from jax.experimental import pallas as pl
from jax.experimental.pallas import tpu as pltpu

F32 = jnp.float32
BF16 = jnp.bfloat16

HEAD_DIM = 64
N_HEADS = 8
ATT_W = N_HEADS * HEAD_DIM
LANES = 128
HEADS_PER_TILE = LANES // HEAD_DIM
N_PAIRS = ATT_W // LANES
MOBA_BLOCK = 256
MOBA_TOPK = 3
PAGE_SIZE = 128
PAGES_PER_MOBA_BLOCK = MOBA_BLOCK // PAGE_SIZE
PAGES_PER_STEP = 16
LRU_W = 512
LRU_BLOCKS = 8
CONV_W = 4
RG_C = 8.0
N_BUCKETS = 32
MAX_DIST = 128
RMS_EPS = 1e-6
NEG = -1e30
SCALE = HEAD_DIM ** -0.5
VMEM_LIMIT = 56 * 1024 * 1024

_NT = (((1,), (1,)), ((), ()))


def _params(*sem):
    return pltpu.CompilerParams(dimension_semantics=sem, vmem_limit_bytes=VMEM_LIMIT)


def _rms(x, g):
    return x * lax.rsqrt(jnp.mean(x * x, axis=-1, keepdims=True) + RMS_EPS) * g


def _softplus(z):
    return jnp.maximum(z, 0.0) + jnp.log1p(jnp.exp(-jnp.abs(z)))


def _softplus_abs(z):
    return jnp.maximum(z, 0.0) + jnp.log(1.0 + jnp.exp(-jnp.abs(z)))


def _one_minus_exp_2x(x):
    t = jnp.tanh(x)
    return -2.0 * t / (1.0 - t)


def _dot_nt(a, b):
    return lax.dot_general(a.astype(BF16), b.astype(BF16), _NT, preferred_element_type=F32)


def _ffn_kernel(x_ref, g_ref, w1_ref, w3_ref, w2_ref, gf_ref, o_ref, h_ref, acc_ref, *, final_norm):
    j = pl.program_id(1)

    @pl.when(j == 0)
    def _():
        h_ref[...] = _rms(x_ref[...], g_ref[...]).astype(BF16)
        acc_ref[...] = jnp.zeros_like(acc_ref)

    h = h_ref[...]
    a = jnp.dot(h, w1_ref[...], preferred_element_type=F32)
    b = jnp.dot(h, w3_ref[...], preferred_element_type=F32)
    u = a * jax.nn.sigmoid(a) * b
    acc_ref[...] += jnp.dot(u.astype(BF16), w2_ref[...], preferred_element_type=F32)

    @pl.when(j == pl.num_programs(1) - 1)
    def _():
        y = x_ref[...] + 0.5 * acc_ref[...]
        if final_norm:
            y = _rms(y, gf_ref[...])
        o_ref[...] = y


def _ffn(x, g, w1, w3, w2, gf, *, layer, final_norm, tm, tf):
    n, d = x.shape
    dff = w1.shape[2]
    return pl.pallas_call(
        functools.partial(_ffn_kernel, final_norm=final_norm),
        grid=(n // tm, dff // tf),
        in_specs=[
            pl.BlockSpec((tm, d), lambda i, j: (i, 0)),
            pl.BlockSpec((1, d), lambda i, j: (0, 0)),
            pl.BlockSpec((None, d, tf), lambda i, j: (layer, 0, j)),
            pl.BlockSpec((None, d, tf), lambda i, j: (layer, 0, j)),
            pl.BlockSpec((None, tf, d), lambda i, j: (layer, j, 0)),
            pl.BlockSpec((1, d), lambda i, j: (0, 0)),
        ],
        out_specs=pl.BlockSpec((tm, d), lambda i, j: (i, 0)),
        out_shape=jax.ShapeDtypeStruct((n, d), F32),
        scratch_shapes=[pltpu.VMEM((tm, d), BF16), pltpu.VMEM((tm, d), F32)],
        compiler_params=_params("parallel", "arbitrary"),
        name="ffn",
    )(x, g, w1, w3, w2, gf)


N_IN_SPLIT = 8
K_SLOTS = (1, 6)
V_SLOTS = (2, 7)


def _inproj_kernel(x_ref, g_ref, w_ref, *refs, prompt_layouts):
    outs, extra, h_ref = refs[:N_IN_SPLIT], refs[N_IN_SPLIT:-1], refs[-1]
    j = pl.program_id(1)

    @pl.when(j == 0)
    def _():
        h_ref[...] = _rms(x_ref[...], g_ref[...]).astype(BF16)

    for k in range(N_IN_SPLIT // 2):
        @pl.when(j == k)
        def _(k=k):
            p = jnp.dot(h_ref[...], w_ref[...], preferred_element_type=F32)
            for slot in (2 * k, 2 * k + 1):
                part = p[:, (slot % 2) * ATT_W:(slot % 2 + 1) * ATT_W]
                if not prompt_layouts or slot not in K_SLOTS + V_SLOTS:
                    outs[slot][...] = part
                    continue
                part_t = part.T
                outs[slot][...] = part_t
                if slot in K_SLOTS:
                    extra[2 * K_SLOTS.index(slot)][...] = part.astype(BF16)
                    if slot == K_SLOTS[0]:
                        kmean_ref = extra[-1]
                        for blk in range(kmean_ref.shape[0]):
                            kmean_ref[blk:blk + 1, :] = jnp.mean(
                                part[blk * MOBA_BLOCK:(blk + 1) * MOBA_BLOCK, :], axis=0, keepdims=True)
                else:
                    vt_ref = extra[2 * V_SLOTS.index(slot) + 1]
                    for blk in range(vt_ref.shape[0]):
                        vt_ref[blk] = part_t[:, blk * MOBA_BLOCK:(blk + 1) * MOBA_BLOCK].astype(BF16)


def _inproj(x, g, w_in, *, layer, tm, seq=None):
    n, d = x.shape
    tn = 2 * ATT_W
    out_specs = [pl.BlockSpec((tm, ATT_W), lambda i, j: (i, 0)) for _ in range(N_IN_SPLIT)]
    out_shape = [jax.ShapeDtypeStruct((n, ATT_W), F32) for _ in range(N_IN_SPLIT)]
    if seq is not None:
        tiles_per_seq = seq // tm
        blocks_per_tile = tm // MOBA_BLOCK
        for slot in K_SLOTS + V_SLOTS:
            out_specs[slot] = pl.BlockSpec((None, ATT_W, tm),
                                           lambda i, j: (i // tiles_per_seq, 0, i % tiles_per_seq))
            out_shape[slot] = jax.ShapeDtypeStruct((n // seq, ATT_W, seq), F32)
        for _ in range(2):
            out_specs.append(pl.BlockSpec((tm, ATT_W), lambda i, j: (i, 0)))
            out_shape.append(jax.ShapeDtypeStruct((n, ATT_W), BF16))
            out_specs.append(pl.BlockSpec((blocks_per_tile, ATT_W, MOBA_BLOCK), lambda i, j: (i, 0, 0)))
            out_shape.append(jax.ShapeDtypeStruct((n // MOBA_BLOCK, ATT_W, MOBA_BLOCK), BF16))
        out_specs.append(pl.BlockSpec((None, blocks_per_tile, ATT_W), lambda i, j: (i, 0, 0)))
        out_shape.append(jax.ShapeDtypeStruct((n // tm, blocks_per_tile, ATT_W), F32))
    return pl.pallas_call(
        functools.partial(_inproj_kernel, prompt_layouts=seq is not None),
        grid=(n // tm, N_IN_SPLIT * ATT_W // tn),
        in_specs=[
            pl.BlockSpec((tm, d), lambda i, j: (i, 0)),
            pl.BlockSpec((1, d), lambda i, j: (0, 0)),
            pl.BlockSpec((None, d, tn), lambda i, j: (layer, 0, j)),
        ],
        out_specs=out_specs,
        out_shape=out_shape,
        scratch_shapes=[pltpu.VMEM((tm, d), BF16)],
        compiler_params=_params("parallel", "arbitrary"),
        name="inproj",
    )(x, g, w_in)


N_BRANCH = 3


def _merge_kernel(oa_ref, ob_ref, oc_ref, x_ref, g_ref, bg_ref, *refs, transposed_attention):
    wg_refs, (wb_ref, wo_ref, o_ref) = refs[:N_BRANCH], refs[N_BRANCH:]
    x = x_ref[...]
    h = _rms(x, g_ref[...]).astype(BF16)
    merged = None
    for n, o_n in enumerate((oa_ref, ob_ref, oc_ref)):
        branch = o_n[...]
        if transposed_attention and n != 1:
            branch = branch.T
        proj = jnp.dot(branch.astype(BF16), wb_ref[n], preferred_element_type=F32)
        logits = jnp.dot(h, wg_refs[n][...], preferred_element_type=F32)
        gate = jax.nn.sigmoid(logits + bg_ref[n:n + 1, :])
        merged = gate * proj if merged is None else merged + gate * proj
    o_ref[...] = x + jnp.dot(merged.astype(BF16), wo_ref[...], preferred_element_type=F32)


def _merge(oa, ob, oc, x, g, bg, w_in, wb, wo, *, layer, tm, transposed_attention):
    n, d = x.shape
    first_gate_tile = w_in.shape[2] // d - N_BRANCH
    row = lambda w: pl.BlockSpec((tm, w), lambda i: (i, 0))
    att = pl.BlockSpec((None, ATT_W, tm), lambda i: (i, 0, 0)) if transposed_attention else row(ATT_W)
    gate_w = [pl.BlockSpec((None, d, d), lambda i, t=first_gate_tile + k: (layer, 0, t))
              for k in range(N_BRANCH)]
    return pl.pallas_call(
        functools.partial(_merge_kernel, transposed_attention=transposed_attention),
        grid=(n // tm,),
        in_specs=[att, row(ATT_W), att, row(d),
                  pl.BlockSpec((1, d), lambda i: (0, 0)),
                  pl.BlockSpec(bg.shape, lambda i: (0, 0)),
                  *gate_w,
                  pl.BlockSpec((None,) + wb.shape[1:], lambda i: (layer, 0, 0, 0)),
                  pl.BlockSpec((None,) + wo.shape[1:], lambda i: (layer, 0, 0))],
        out_specs=row(d),
        out_shape=jax.ShapeDtypeStruct((n, d), F32),
        compiler_params=_params("parallel"),
        name="merge",
    )(oa, ob, oc, x, g, bg, *([w_in] * N_BRANCH), wb, wo)


def _rel_bucket(dist):
    max_exact = N_BUCKETS // 2
    d = jnp.maximum(dist, 0)
    df = jnp.maximum(d, 1).astype(F32)
    large = max_exact + (jnp.log(df / max_exact) / math.log(MAX_DIST / max_exact)
                         * (N_BUCKETS - max_exact)).astype(jnp.int32)
    large = jnp.minimum(large, N_BUCKETS - 1)
    return jnp.where(d < max_exact, d, large)


def _bias_tables(rel_bias):
    period = 4 * MOBA_BLOCK
    k = jnp.arange(period, dtype=jnp.int32)
    k = jnp.where(k < period // 2, k, k - period)
    u = rel_bias[_rel_bucket(jnp.maximum(MOBA_BLOCK + k, 0))].T
    n_keys = 2 * MOBA_BLOCK
    toeplitz = jnp.tile(u, (1, n_keys))[:, :n_keys * (period - 1)]
    toeplitz = toeplitz.reshape(N_HEADS, n_keys, period - 1)[:, :, :MOBA_BLOCK]
    heads_per_step = STEP_W // HEAD_DIM
    n_steps = N_HEADS // heads_per_step
    toeplitz = toeplitz.reshape(n_steps, heads_per_step, n_keys, MOBA_BLOCK)
    toeplitz = jnp.transpose(toeplitz, (0, 2, 1, 3)).reshape(n_steps, n_keys, STEP_Q)
    far = rel_bias[_rel_bucket(jnp.int32(2 * MOBA_BLOCK))]
    far_pair = jnp.repeat(far.reshape(n_steps, 1, heads_per_step), MOBA_BLOCK, axis=2)
    dec = rel_bias[_rel_bucket(MOBA_BLOCK - jnp.arange(MOBA_BLOCK, dtype=jnp.int32))].T
    own0 = rel_bias[_rel_bucket(jnp.int32(0))][:, None]
    return toeplitz, far_pair, dec, far[:, None], own0


PAIR_Q = HEADS_PER_TILE * MOBA_BLOCK
TILES_PER_STEP = 4
STEP_W = TILES_PER_STEP * LANES
STEP_Q = TILES_PER_STEP * PAIR_Q


def _stacked_queries(q):
    lane = lax.broadcasted_iota(jnp.int32, (1, LANES), 1)
    stacks = []
    for g in range(TILES_PER_STEP):
        qs = q[:, g * LANES:(g + 1) * LANES] * SCALE
        stacks.append(jnp.concatenate([jnp.where((lane // HEAD_DIM) == hh, qs, 0.0).astype(BF16)
                                       for hh in range(HEADS_PER_TILE)], axis=0))
    return stacks


def _scores(k_blk, qcat):
    return jnp.concatenate([_dot_nt(k_blk[:, g * LANES:(g + 1) * LANES], qcat[g])
                            for g in range(TILES_PER_STEP)], axis=1)


def _values(vt_blk, w):
    return jnp.concatenate([jnp.dot(vt_blk[g * LANES:(g + 1) * LANES, :], w[:, g * PAIR_Q:(g + 1) * PAIR_Q],
                                    preferred_element_type=F32)
                            for g in range(TILES_PER_STEP)], axis=0)


def _per_head_rows(stat):
    return jnp.concatenate([jnp.broadcast_to(stat[:, g * PAIR_Q:(g + 1) * PAIR_Q], (LANES, PAIR_Q))
                            for g in range(TILES_PER_STEP)], axis=0)


def _pair_corners(acc):
    rows = []
    for r in range(STEP_W // HEAD_DIM):
        hh = r % HEADS_PER_TILE
        rows.append(acc[r * HEAD_DIM:(r + 1) * HEAD_DIM, hh * MOBA_BLOCK:(hh + 1) * MOBA_BLOCK])
    return jnp.concatenate(rows, axis=0)


def _moba_prompt_kernel(q_ref, kmean_ref, kb_ref, vt_ref, tb_ref, far_ref, o_ref, sel_ref):
    i = pl.program_id(2)
    nb = kb_ref.shape[0] // MOBA_BLOCK
    q = q_ref[...]
    qcat = _stacked_queries(q)
    lane = lax.broadcasted_iota(jnp.int32, (1, LANES), 1)
    blk_row = lax.broadcasted_iota(jnp.int32, (nb, 1), 0)

    gate = jnp.concatenate(
        [lax.dot_general(jnp.where((lane // HEAD_DIM) == hh, kmean_ref[:, g * LANES:(g + 1) * LANES], 0.0),
                         q[:, g * LANES:(g + 1) * LANES], _NT,
                         precision=lax.Precision.HIGHEST, preferred_element_type=F32)
         for g in range(TILES_PER_STEP) for hh in range(HEADS_PER_TILE)], axis=1)
    gate = jnp.where(blk_row < i, gate, NEG)
    rank = jnp.zeros_like(gate)
    for m in range(nb):
        gm = gate[m:m + 1, :]
        ahead = (gm > gate) | ((gm == gate) & (blk_row > m))
        rank = rank + ahead.astype(F32)
    sel_ref[...] = ((rank < MOBA_TOPK) & (blk_row < i)).astype(F32)

    def block(n, bias, keep, m_run, l_run, acc):
        start = pl.multiple_of(n * MOBA_BLOCK, MOBA_BLOCK)
        s = _scores(kb_ref[pl.ds(start, MOBA_BLOCK), :], qcat) + bias
        s = jnp.where(keep, s, NEG)
        m_new = jnp.max(s, axis=0, keepdims=True)
        if m_run is not None:
            m_new = jnp.maximum(m_run, m_new)
        p = jnp.exp(s - m_new)
        l_new = jnp.sum(p, axis=0, keepdims=True)
        pv = _values(vt_ref[n], p.astype(BF16))
        if m_run is None:
            return m_new, l_new, pv
        alpha = jnp.exp(m_run - m_new)
        return m_new, alpha * l_run + l_new, _per_head_rows(alpha) * acc + pv

    key_i = lax.broadcasted_iota(jnp.int32, (MOBA_BLOCK, STEP_Q), 0)
    qry_i = lax.broadcasted_iota(jnp.int32, (MOBA_BLOCK, STEP_Q), 1) % MOBA_BLOCK
    state0 = block(i, tb_ref[MOBA_BLOCK:, :], key_i <= qry_i, None, None, None)

    def body(n, state):
        bias = jnp.where(n == i - 1, tb_ref[:MOBA_BLOCK, :], far_ref[...])
        return block(n, bias, sel_ref[pl.ds(n, 1), :] > 0.5, *state)

    _, l_fin, acc_fin = lax.fori_loop(0, i, body, state0)
    o_ref[...] = _pair_corners(acc_fin / _per_head_rows(l_fin))


def _moba_prompt(q, k_means, k_bf, vt, toeplitz, far, *, batch, seq):
    nb = seq // MOBA_BLOCK
    return pl.pallas_call(
        _moba_prompt_kernel,
        grid=(batch, ATT_W // STEP_W, nb),
        in_specs=[
            pl.BlockSpec((MOBA_BLOCK, STEP_W), lambda b, p, i: (b * nb + i, p)),
            pl.BlockSpec((nb, STEP_W), lambda b, p, i: (b, p)),
            pl.BlockSpec((seq, STEP_W), lambda b, p, i: (b, p)),
            pl.BlockSpec((nb, STEP_W, MOBA_BLOCK), lambda b, p, i: (b, p, 0)),
            pl.BlockSpec((None, 2 * MOBA_BLOCK, STEP_Q), lambda b, p, i: (p, 0, 0)),
            pl.BlockSpec((None, 1, STEP_Q), lambda b, p, i: (p, 0, 0)),
        ],
        out_specs=pl.BlockSpec((None, STEP_W, MOBA_BLOCK), lambda b, p, i: (b * nb + i, p, 0)),
        out_shape=jax.ShapeDtypeStruct((batch * nb, ATT_W, MOBA_BLOCK), F32),
        scratch_shapes=[pltpu.VMEM((nb, STEP_Q), F32)],
        compiler_params=_params("parallel", "parallel", "arbitrary"),
        name="moba_prompt",
    )(q, k_means, k_bf, vt, toeplitz, far)


SB_BLOCK = 256
F32_EXP_UNDERFLOW = -105.0


def _sb_prompt_kernel(q_ref, kb_ref, vt_ref, o_ref):
    i = pl.program_id(2)
    qcat = _stacked_queries(q_ref[...])
    row_i = lax.broadcasted_iota(jnp.int32, (SB_BLOCK, SB_BLOCK), 0)
    col_i = lax.broadcasted_iota(jnp.int32, (SB_BLOCK, SB_BLOCK), 1)
    later = (row_i < col_i).astype(BF16)

    def block(n, mask, carry):
        start = pl.multiple_of(n * SB_BLOCK, SB_BLOCK)
        z = _scores(kb_ref[pl.ds(start, SB_BLOCK), :], qcat)
        sp = _softplus_abs(z)
        log_keep = -sp if mask is None else jnp.where(mask, -sp, 0.0)
        hi = log_keep.astype(BF16)
        lo = (log_keep - hi.astype(F32)).astype(BF16)
        both = jnp.dot(later, jnp.concatenate([hi, lo], axis=1), preferred_element_type=F32)
        between = both[:, :STEP_Q] + both[:, STEP_Q:]
        if carry is not None:
            between = between + carry
        w = jnp.exp(z - sp + between)
        if mask is not None:
            w = jnp.where(mask, w, 0.0)
        pv = _values(vt_ref[n], w.astype(BF16))
        return jnp.sum(log_keep, axis=0, keepdims=True), pv

    key_i = lax.broadcasted_iota(jnp.int32, (SB_BLOCK, STEP_Q), 0)
    qry_i = lax.broadcasted_iota(jnp.int32, (SB_BLOCK, STEP_Q), 1) % SB_BLOCK
    state0 = block(i, key_i < qry_i, None)

    def more(state):
        t, carry, _ = state
        return jnp.logical_and(t < i, jnp.max(carry) > F32_EXP_UNDERFLOW)

    def body(state):
        t, carry, acc = state
        kept, pv = block(i - 1 - t, None, carry)
        return t + 1, carry + kept, acc + pv

    _, _, acc_fin = lax.while_loop(more, body, (jnp.int32(0),) + state0)
    o_ref[...] = _pair_corners(acc_fin)


def _sb_prompt(q, k_bf, vt, *, batch, seq):
    nb = seq // SB_BLOCK
    return pl.pallas_call(
        _sb_prompt_kernel,
        grid=(batch, ATT_W // STEP_W, nb),
        in_specs=[
            pl.BlockSpec((SB_BLOCK, STEP_W), lambda b, p, i: (b * nb + i, p)),
            pl.BlockSpec((seq, STEP_W), lambda b, p, i: (b, p)),
            pl.BlockSpec((nb, STEP_W, SB_BLOCK), lambda b, p, i: (b, p, 0)),
        ],
        out_specs=pl.BlockSpec((None, STEP_W, SB_BLOCK), lambda b, p, i: (b * nb + i, p, 0)),
        out_shape=jax.ShapeDtypeStruct((batch * nb, ATT_W, SB_BLOCK), F32),
        compiler_params=_params("parallel", "parallel", "arbitrary"),
        name="sb_prompt",
    )(q, k_bf, vt)


def _lru_gates(xc, wa_ref, ba_ref, wx_ref, bx_ref, lam_ref):
    xc_bf = xc.astype(BF16)
    r = jax.nn.sigmoid(jnp.dot(xc_bf, wa_ref[...], preferred_element_type=F32) + ba_ref[...])
    gi = jax.nn.sigmoid(jnp.dot(xc_bf, wx_ref[...], preferred_element_type=F32) + bx_ref[...])
    log_a = -RG_C * r * _softplus(-lam_ref[...])
    a = jnp.exp(log_a)
    b = jnp.sqrt(_one_minus_exp_2x(log_a)) * (gi * xc)
    return a, b


def _lru_prompt_kernel(x_ref, g_ref, buf_ref, h0_ref, cw_ref, cb_ref, wa_ref, ba_ref, wx_ref, bx_ref,
                       lam_ref, y_ref, hlast_ref, conv_ref, tail_ref, hc_ref):
    t = pl.program_id(1)
    tc = x_ref.shape[0]
    tail_rows = tail_ref.shape[0]

    @pl.when(t == 0)
    def _():
        tail_ref[...] = jnp.zeros_like(tail_ref)
        tail_ref[tail_rows - (CONV_W - 1):, :] = buf_ref[...]
        hc_ref[...] = h0_ref[...]

    x = x_ref[...]
    xe = jnp.concatenate([tail_ref[...], x], axis=0)
    xc = cb_ref[...] + cw_ref[CONV_W - 1:CONV_W, :] * x
    for k in range(1, CONV_W):
        xc = xc + cw_ref[CONV_W - 1 - k:CONV_W - k, :] * pltpu.roll(xe, k, 0)[tail_rows:, :]
    a, b = _lru_gates(xc, wa_ref, ba_ref, wx_ref, bx_ref, lam_ref)

    row = lax.broadcasted_iota(jnp.int32, (tc, 1), 0)
    k = 1
    while k < tc:
        a_sh = jnp.where(row >= k, pltpu.roll(a, k, 0), 1.0)
        b_sh = jnp.where(row >= k, pltpu.roll(b, k, 0), 0.0)
        b = a * b_sh + b
        a = a * a_sh
        k *= 2
    h = a * hc_ref[...] + b
    y_ref[...] = h * jax.nn.gelu(g_ref[...], approximate=True)
    hc_ref[...] = h[tc - 1:tc, :]
    tail_ref[...] = x[tc - tail_rows:, :]

    @pl.when(t == pl.num_programs(1) - 1)
    def _():
        hlast_ref[...] = h[tc - 1:tc, :]
        conv_ref[...] = x[tc - (CONV_W - 1):, :]


def _lru_prompt(x, g, buf, h0, lw, *, batch, seq, tc):
    nt = seq // tc
    w = x.shape[1]
    vec = pl.BlockSpec((1, w), lambda b, t: (0, 0))
    mat = pl.BlockSpec((w, w), lambda b, t: (0, 0))
    return pl.pallas_call(
        _lru_prompt_kernel,
        grid=(batch, nt),
        in_specs=[
            pl.BlockSpec((tc, w), lambda b, t: (b * nt + t, 0)),
            pl.BlockSpec((tc, w), lambda b, t: (b * nt + t, 0)),
            pl.BlockSpec((None, CONV_W - 1, w), lambda b, t: (b, 0, 0)),
            pl.BlockSpec((None, 1, w), lambda b, t: (b, 0, 0)),
            pl.BlockSpec((CONV_W, w), lambda b, t: (0, 0)),
            vec, mat, vec, mat, vec, vec,
        ],
        out_specs=[
            pl.BlockSpec((tc, w), lambda b, t: (b * nt + t, 0)),
            pl.BlockSpec((None, 1, w), lambda b, t: (b, 0, 0)),
            pl.BlockSpec((None, CONV_W - 1, w), lambda b, t: (b, 0, 0)),
        ],
        out_shape=[jax.ShapeDtypeStruct(x.shape, F32),
                   jax.ShapeDtypeStruct((batch, 1, w), F32),
                   jax.ShapeDtypeStruct((batch, CONV_W - 1, w), F32)],
        scratch_shapes=[pltpu.VMEM((8, w), F32), pltpu.VMEM((1, w), F32)],
        compiler_params=_params("parallel", "arbitrary"),
        name="lru_prompt",
    )(x, g, buf, h0, lw["conv_w"], lw["conv_b"], lw["wa"], lw["ba"], lw["wx"], lw["bx"], lw["lam"])


def _lru_step_kernel(x_ref, g_ref, buf_ref, h0_ref, cw_ref, cb_ref, wa_ref, ba_ref, wx_ref, bx_ref,
                     lam_ref, y_ref, h_ref, conv_ref):
    x = x_ref[...]
    xc = cb_ref[...] + cw_ref[CONV_W - 1:CONV_W, :] * x
    for k in range(CONV_W - 1):
        xc = xc + cw_ref[k:k + 1, :] * buf_ref[k]
    a, b = _lru_gates(xc, wa_ref, ba_ref, wx_ref, bx_ref, lam_ref)
    h = a * h0_ref[...] + b
    y_ref[...] = h * jax.nn.gelu(g_ref[...], approximate=True)
    h_ref[...] = h
    for k in range(CONV_W - 2):
        conv_ref[k] = buf_ref[k + 1]
    conv_ref[CONV_W - 2] = x


def _lru_step(x, g, buf, h0, lw):
    n, w = x.shape
    return pl.pallas_call(
        _lru_step_kernel,
        out_shape=[jax.ShapeDtypeStruct((n, w), F32), jax.ShapeDtypeStruct((n, w), F32),
                   jax.ShapeDtypeStruct((CONV_W - 1, n, w), F32)],
        compiler_params=pltpu.CompilerParams(vmem_limit_bytes=VMEM_LIMIT),
        name="lru_step",
    )(x, g, buf, h0, lw["conv_w"], lw["conv_b"], lw["wa"], lw["ba"], lw["wx"], lw["bx"], lw["lam"])


def _page_scores(qb, kt_ref):
    return jnp.concatenate([jnp.sum(qb[h] * kt_ref[h], axis=0, keepdims=True)
                            for h in range(N_HEADS)], axis=0)


def _weighted_values(w, vt_ref):
    return jnp.stack([jnp.broadcast_to(w[h:h + 1, :], (HEAD_DIM, PAGE_SIZE)) * vt_ref[h]
                      for h in range(N_HEADS)], axis=0)


def _sum_positions(acc):
    ones = jnp.ones((N_HEADS, PAGE_SIZE), F32)
    rows = [lax.dot_general(ones, acc[h], _NT, precision=lax.Precision.HIGHEST,
                            preferred_element_type=F32)[:1] for h in range(N_HEADS)]
    return jnp.concatenate(rows, axis=0)


def _moba_decode_kernel(pt_ref, q_ref, q2_ref, kn_ref, vn_ref, dec_ref, far_ref, own_ref, *refs):
    del pt_ref
    n_pg = PAGES_PER_STEP
    k_pages, v_pages = refs[:n_pg], refs[n_pg:2 * n_pg]
    o_ref, gate_ref, m_ref, l_ref, acc_ref = refs[2 * n_pg:]
    c = pl.program_id(1)
    n_steps = pl.num_programs(1)
    blocks_per_step = n_pg // PAGES_PER_MOBA_BLOCK
    qb = jnp.broadcast_to(q_ref[...] * SCALE, (N_HEADS, HEAD_DIM, PAGE_SIZE))
    lane = lax.broadcasted_iota(jnp.int32, (1, LANES), 1)

    @pl.when(c == 0)
    def _():
        gate_ref[...] = jnp.full_like(gate_ref, NEG)
        m_ref[...] = jnp.full_like(m_ref, NEG)
        l_ref[...] = jnp.zeros_like(l_ref)

    for j in range(blocks_per_step):
        n = c * blocks_per_step + j
        pages = range(PAGES_PER_MOBA_BLOCK * j, PAGES_PER_MOBA_BLOCK * (j + 1))
        z = [_page_scores(qb, k_pages[pg]) for pg in pages]
        g = sum(jnp.sum(zt, axis=-1, keepdims=True) for zt in z) * (1.0 / (SCALE * MOBA_BLOCK))
        newest = n == n_steps * blocks_per_step - 1
        s = [zt + jnp.where(newest, dec_ref[:, t * PAGE_SIZE:(t + 1) * PAGE_SIZE], far_ref[...])
             for t, zt in enumerate(z)]
        m = functools.reduce(jnp.maximum, [jnp.max(st, axis=-1, keepdims=True) for st in s])
        p = [jnp.exp(st - m) for st in s]
        l = sum(jnp.sum(pt, axis=-1, keepdims=True) for pt in p)
        acc_ref[n] = sum(_weighted_values(pt, v_pages[pg]) for pt, pg in zip(p, pages))
        gate_ref[...] = jnp.where(lane == n, g, gate_ref[...])
        m_ref[...] = jnp.where(lane == n, m, m_ref[...])
        l_ref[...] = jnp.where(lane == n, l, l_ref[...])

    @pl.when(c == n_steps - 1)
    def _():
        nb = n_steps * blocks_per_step
        gate = gate_ref[...]
        rank = jnp.zeros_like(gate)
        for mm in range(nb):
            gm = gate[:, mm:mm + 1]
            ahead = (gm > gate) | ((gm == gate) & (lane > mm))
            rank = rank + ahead.astype(F32)
        sel = (rank < MOBA_TOPK) & (lane < nb)
        s_own = jnp.sum(q2_ref[...] * SCALE * kn_ref[...], axis=-1, keepdims=True) + own_ref[...]
        m_sel = jnp.where(sel, m_ref[...], NEG)
        m_tot = jnp.maximum(jnp.max(m_sel, axis=-1, keepdims=True), s_own)
        wgt = jnp.where(sel, jnp.exp(m_sel - m_tot), 0.0)
        w_own = jnp.exp(s_own - m_tot)
        l_tot = jnp.sum(wgt * l_ref[...], axis=-1, keepdims=True) + w_own
        merged = []
        for h in range(N_HEADS):
            tot = jnp.zeros((HEAD_DIM, PAGE_SIZE), F32)
            for mm in range(nb):
                tot = tot + jnp.broadcast_to(wgt[h:h + 1, mm:mm + 1], (HEAD_DIM, PAGE_SIZE)) * acc_ref[mm, h]
            merged.append(tot)
        o = w_own * vn_ref[...] + _sum_positions(merged)
        o_ref[...] = o / l_tot


def _page_spec(layer, slot):
    def index_map(b, c, pt):
        return (layer, pt[b, c * PAGES_PER_STEP + slot], 0, 0, 0)
    return pl.BlockSpec((None, None, N_HEADS, HEAD_DIM, PAGE_SIZE), index_map)


def _moba_decode(page_table, q, k_new, v_new, cache_k, cache_v, dec, far, own0, *, layer):
    batch, n_pages = page_table.shape
    n_steps = n_pages // PAGES_PER_STEP
    nb = n_pages // PAGES_PER_MOBA_BLOCK
    heads = pl.BlockSpec((None, N_HEADS, HEAD_DIM), lambda b, c, pt: (b, 0, 0))
    column = pl.BlockSpec((None, N_HEADS, HEAD_DIM, 1), lambda b, c, pt: (b, 0, 0, 0))
    const = lambda a: pl.BlockSpec(a.shape, lambda b, c, pt: (0,) * a.ndim)
    grid_spec = pltpu.PrefetchScalarGridSpec(
        num_scalar_prefetch=1,
        grid=(batch, n_steps),
        in_specs=[column, heads, heads, heads, const(dec), const(far), const(own0)]
        + [_page_spec(layer, s) for s in range(PAGES_PER_STEP)]
        + [_page_spec(layer, s) for s in range(PAGES_PER_STEP)],
        out_specs=heads,
        scratch_shapes=[pltpu.VMEM((N_HEADS, LANES), F32), pltpu.VMEM((N_HEADS, LANES), F32),
                        pltpu.VMEM((N_HEADS, LANES), F32),
                        pltpu.VMEM((nb, N_HEADS, HEAD_DIM, PAGE_SIZE), F32)],
    )
    return pl.pallas_call(
        _moba_decode_kernel,
        grid_spec=grid_spec,
        out_shape=jax.ShapeDtypeStruct((batch, N_HEADS, HEAD_DIM), F32),
        compiler_params=_params("parallel", "arbitrary"),
        name="moba_decode",
    )(page_table, q[..., None], q, k_new, v_new, dec, far, own0,
      *([cache_k] * PAGES_PER_STEP), *([cache_v] * PAGES_PER_STEP))


SB_DECODE_SLOTS = 3


def _sb_decode_kernel(pt_ref, q_ref, k_hbm, v_hbm, o_ref, kbuf, vbuf, sem, *, layer):
    b = pl.program_id(0)
    n_pages = pt_ref.shape[1]
    ahead = SB_DECODE_SLOTS - 1
    qb = jnp.broadcast_to(q_ref[...] * SCALE, (N_HEADS, HEAD_DIM, PAGE_SIZE))
    lane = lax.broadcasted_iota(jnp.int32, (1, PAGE_SIZE), 1)

    def page_copies(j):
        page = pt_ref[b, n_pages - 1 - j]
        slot = j % SB_DECODE_SLOTS
        return (pltpu.make_async_copy(k_hbm.at[layer, page], kbuf.at[slot], sem.at[0, slot]),
                pltpu.make_async_copy(v_hbm.at[layer, page], vbuf.at[slot], sem.at[1, slot]))

    def start(j):
        for copy in page_copies(j):
            copy.start()

    def wait(j):
        for copy in page_copies(j):
            copy.wait()

    for j in range(ahead):
        start(j)

    def more(state):
        j, carry, _ = state
        return jnp.logical_and(j < n_pages, jnp.max(carry) > F32_EXP_UNDERFLOW)

    def body(state):
        j, carry, acc = state
        slot = j % SB_DECODE_SLOTS
        wait(j)

        @pl.when(j + ahead < n_pages)
        def _():
            start(j + ahead)

        z = _page_scores(qb, kbuf.at[slot])
        sp = _softplus_abs(z)
        log_keep = -sp
        incl = log_keep
        shift = 1
        while shift < PAGE_SIZE:
            moved = pltpu.roll(incl, PAGE_SIZE - shift, 1)
            incl = incl + jnp.where(lane < PAGE_SIZE - shift, moved, 0.0)
            shift *= 2
        w = jnp.exp(z - sp + incl - log_keep + carry)
        acc = acc + _weighted_values(w, vbuf.at[slot])
        return j + 1, carry + jnp.sum(log_keep, axis=-1, keepdims=True), acc

    state0 = (jnp.int32(0), jnp.zeros((N_HEADS, 1), F32), jnp.zeros((N_HEADS, HEAD_DIM, PAGE_SIZE), F32))
    j_end, _, acc = lax.while_loop(more, body, state0)

    for k in range(ahead):
        @pl.when(j_end + k < n_pages)
        def _(k=k):
            wait(j_end + k)

    o_ref[...] = _sum_positions(acc)


def _sb_decode(page_table, q, cache_k, cache_v, *, layer):
    batch, n_pages = page_table.shape
    assert n_pages >= SB_DECODE_SLOTS - 1
    page_buffer = pltpu.VMEM((SB_DECODE_SLOTS, N_HEADS, HEAD_DIM, PAGE_SIZE), F32)
    grid_spec = pltpu.PrefetchScalarGridSpec(
        num_scalar_prefetch=1,
        grid=(batch,),
        in_specs=[pl.BlockSpec((None, N_HEADS, HEAD_DIM, 1), lambda b, pt: (b, 0, 0, 0)),
                  pl.BlockSpec(memory_space=pl.ANY),
                  pl.BlockSpec(memory_space=pl.ANY)],
        out_specs=pl.BlockSpec((None, N_HEADS, HEAD_DIM), lambda b, pt: (b, 0, 0)),
        scratch_shapes=[page_buffer, page_buffer, pltpu.SemaphoreType.DMA((2, SB_DECODE_SLOTS))],
    )
    return pl.pallas_call(
        functools.partial(_sb_decode_kernel, layer=layer),
        grid_spec=grid_spec,
        out_shape=jax.ShapeDtypeStruct((batch, N_HEADS, HEAD_DIM), F32),
        compiler_params=_params("arbitrary"),
        name="sb_decode",
    )(page_table, q[..., None], cache_k, cache_v)


def _block_diag(w):
    nblk, bk, bj = w.shape
    eye = jnp.eye(nblk, dtype=w.dtype)
    return jnp.einsum("nkj,nm->nkmj", w, eye).reshape(nblk * bk, nblk * bj)


def _layer_weights(l, norm_ffn1, ffn1_w1, ffn1_w3, ffn1_w2, norm_mix, w_in, b_gate, conv_w, conv_b,
                   lru_wa, lru_ba, lru_wx, lru_bx, lru_lambda, w_branch, w_out, norm_ffn2,
                   ffn2_w1, ffn2_w3, ffn2_w2):
    vec = lambda a: a[l].reshape(1, -1)
    bf = lambda a: a.astype(BF16)
    return {
        "ln1": vec(norm_ffn1), "f1_w1": bf(ffn1_w1), "f1_w3": bf(ffn1_w3), "f1_w2": bf(ffn1_w2),
        "ln_mix": vec(norm_mix), "w_in": bf(w_in), "b_gate": b_gate[l],
        "lru": {"conv_w": conv_w[l], "conv_b": vec(conv_b),
                "wa": _block_diag(lru_wa[l]).astype(BF16), "ba": vec(lru_ba),
                "wx": _block_diag(lru_wx[l]).astype(BF16), "bx": vec(lru_bx), "lam": vec(lru_lambda)},
        "w_branch": bf(w_branch), "w_out": bf(w_out),
        "ln2": vec(norm_ffn2), "f2_w1": bf(ffn2_w1), "f2_w3": bf(ffn2_w3), "f2_w2": bf(ffn2_w2),
    }


FFN_TILE_F = 256


def kernel(x_prompt, x_sample, cache_moba_k, cache_moba_v, cache_sb_k, cache_sb_v, state_lru_h,
           state_lru_conv, page_table, rel_bias, norm_ffn1, ffn1_w1, ffn1_w3, ffn1_w2, norm_mix, w_in,
           b_gate, conv_w, conv_b, lru_wa, lru_ba, lru_wx, lru_bx, lru_lambda, w_branch, w_out,
           norm_ffn2, ffn2_w1, ffn2_w3, ffn2_w2, norm_final):
    bp, seq, d = x_prompt.shape
    db = x_sample.shape[0]
    depth = w_in.shape[0]
    n_phys = cache_moba_k.shape[1]
    toeplitz, far, dec, far_dec, own0 = _bias_tables(rel_bias)
    gf = norm_final.reshape(1, d)
    paged = lambda c: jnp.transpose(c, (0, 1, 3, 4, 2))
    cmk, cmv, csk, csv = paged(cache_moba_k), paged(cache_moba_v), paged(cache_sb_k), paged(cache_sb_v)

    yp = x_prompt.reshape(bp * seq, d)
    ys = x_sample.reshape(db, d)
    tm_p = 1024 if (bp * seq) % 1024 == 0 else MOBA_BLOCK
    zeros_buf = jnp.zeros((bp, CONV_W - 1, LRU_W), F32)
    zeros_h = jnp.zeros((bp, 1, LRU_W), F32)
    outs = {k: [] for k in ("p_mk", "p_mv", "p_sk", "p_sv", "p_h", "p_cv",
                            "s_mk", "s_mv", "s_sk", "s_sv", "s_h", "s_cv")}
    for l in range(depth):
        lw = _layer_weights(l, norm_ffn1, ffn1_w1, ffn1_w3, ffn1_w2, norm_mix, w_in, b_gate, conv_w,
                            conv_b, lru_wa, lru_ba, lru_wx, lru_bx, lru_lambda, w_branch, w_out,
                            norm_ffn2, ffn2_w1, ffn2_w3, ffn2_w2)
        last = l == depth - 1

        yp = _ffn(yp, lw["ln1"], lw["f1_w1"], lw["f1_w3"], lw["f1_w2"], gf,
                  layer=l, final_norm=False, tm=tm_p, tf=FFN_TILE_F)
        qa, ka, va, xb, gb, qc, kc, vc, ka_bf, vat, kc_bf, vct, ka_means = _inproj(
            yp, lw["ln_mix"], lw["w_in"], layer=l, tm=max(tm_p // 2, MOBA_BLOCK), seq=seq)
        o_a = _moba_prompt(qa, ka_means.reshape(bp * seq // MOBA_BLOCK, ATT_W), ka_bf, vat, toeplitz, far,
                           batch=bp, seq=seq)
        o_b, h_new, conv_new = _lru_prompt(xb, gb, zeros_buf, zeros_h, lw["lru"],
                                           batch=bp, seq=seq, tc=min(seq, 512))
        o_c = _sb_prompt(qc, kc_bf, vct, batch=bp, seq=seq)
        yp = _merge(o_a, o_b, o_c, yp, lw["ln_mix"], lw["b_gate"], lw["w_in"], lw["w_branch"], lw["w_out"],
                    layer=l, tm=MOBA_BLOCK, transposed_attention=True)
        yp = _ffn(yp, lw["ln2"], lw["f2_w1"], lw["f2_w3"], lw["f2_w2"], gf,
                  layer=l, final_norm=last, tm=tm_p, tf=FFN_TILE_F)
        outs["p_mk"].append(ka); outs["p_mv"].append(va)
        outs["p_sk"].append(kc); outs["p_sv"].append(vc)
        outs["p_h"].append(h_new.reshape(bp, LRU_W)); outs["p_cv"].append(conv_new)

        ys = _ffn(ys, lw["ln1"], lw["f1_w1"], lw["f1_w3"], lw["f1_w2"], gf,
                  layer=l, final_norm=False, tm=db, tf=FFN_TILE_F)
        qa, ka, va, xb, gb, qc, kc, vc = _inproj(ys, lw["ln_mix"], lw["w_in"], layer=l, tm=db)
        by_head = lambda a: a.reshape(db, N_HEADS, HEAD_DIM)
        o_a = _moba_decode(page_table, by_head(qa), by_head(ka), by_head(va), cmk, cmv,
                           dec, far_dec, own0, layer=l).reshape(db, ATT_W)
        o_b, h_new, conv_new = _lru_step(xb, gb, jnp.transpose(state_lru_conv[l], (1, 0, 2)),
                                         state_lru_h[l], lw["lru"])
        o_c = _sb_decode(page_table, by_head(qc), csk, csv, layer=l).reshape(db, ATT_W)
        ys = _merge(o_a, o_b, o_c, ys, lw["ln_mix"], lw["b_gate"], lw["w_in"], lw["w_branch"], lw["w_out"],
                    layer=l, tm=db, transposed_attention=False)
        ys = _ffn(ys, lw["ln2"], lw["f2_w1"], lw["f2_w3"], lw["f2_w2"], gf,
                  layer=l, final_norm=last, tm=db, tf=FFN_TILE_F)
        kv_shape = (db, 1, N_HEADS, HEAD_DIM)
        outs["s_mk"].append(ka.reshape(kv_shape)); outs["s_mv"].append(va.reshape(kv_shape))
        outs["s_sk"].append(kc.reshape(kv_shape)); outs["s_sv"].append(vc.reshape(kv_shape))
        outs["s_h"].append(h_new); outs["s_cv"].append(jnp.transpose(conv_new, (1, 0, 2)))

    st = {k: jnp.stack(v) for k, v in outs.items()}
    for k in ("p_mk", "p_mv", "p_sk", "p_sv"):
        st[k] = jnp.transpose(st[k].reshape(depth, bp, N_HEADS, HEAD_DIM, seq), (0, 1, 4, 2, 3))
    return (yp.reshape(bp, seq, d), ys.reshape(db, 1, d),
            st["p_mk"], st["p_mv"], st["p_sk"], st["p_sv"], st["p_h"], st["p_cv"],
            st["s_mk"], st["s_mv"], st["s_sk"], st["s_sv"], st["s_h"], st["s_cv"])
```
